```python
import math
import jax, jax.numpy as jnp
from jax import lax
import numpy as np

D_MODEL = 4096
BATCH = 2
SEQ = 8192
DEPTH = 1
DEC_BATCH = 1
DEC_SEQ = 16384
PAST_LEN = 128

HEAD_DIM = 128
H_A = 16
KV_A = 4
H_B = 16
KV_B = 4
G_A = H_A // KV_A
G_B = H_B // KV_B
W_A = H_A * HEAD_DIM
W_B = H_B * HEAD_DIM
MIX_WIDTH = W_A + W_B
KVW_A = KV_A * HEAD_DIM
KVW_B = KV_B * HEAD_DIM
IN_WIDTH = W_A + 2 * KVW_A + W_B + 2 * KVW_B
Q_BLOCK = 128
WINDOW = 128
GRID_W = 64
HALF_ROT = HEAD_DIM // 2
ROPE_THETA = 10000.0
REL_BUCKETS = 32
REL_MAX_DIST = 128
N_EXPERTS = 16
EC_FACTOR = 2
D_FF_EXPERT = 2048
NORM_EPS = 1e-6
NEG_INF = -1e30

kernel_name = "hybrid_axial_window_ec_encoder"


def rms_norm(x, g):
    xf = x.astype(jnp.float32)
    y = xf * lax.rsqrt(jnp.mean(xf * xf, axis=-1, keepdims=True) + NORM_EPS)
    return (y * g.astype(jnp.float32)).astype(x.dtype)


def rms_norm_f32(x, g):
    xf = x.astype(jnp.float32)
    return xf * lax.rsqrt(jnp.mean(xf * xf, axis=-1, keepdims=True) + NORM_EPS) * g.astype(jnp.float32)


def axial_angles(n):
    rows = n // GRID_W
    row = jnp.repeat(jnp.arange(rows, dtype=jnp.float32), GRID_W)
    col = jnp.tile(jnp.arange(GRID_W, dtype=jnp.float32), rows)
    inv = ROPE_THETA ** (-jnp.arange(0, HALF_ROT, 2, dtype=jnp.float32) / HALF_ROT)
    return row[:, None] * inv, col[:, None] * inv


def rope_half(x, ang):
    c = jnp.cos(ang)[None, :, None, :]
    s = jnp.sin(ang)[None, :, None, :]
    x1, x2 = jnp.split(x, 2, axis=-1)
    return jnp.concatenate([x1 * c - x2 * s, x2 * c + x1 * s], axis=-1)


def axial_rope(x, ang_row, ang_col):
    return jnp.concatenate([rope_half(x[..., :HALF_ROT], ang_row),
                            rope_half(x[..., HALF_ROT:], ang_col)], axis=-1)


def global_axial_attention(q, k, v, qn, kn):
    b, n, _ = q.shape
    dt = q.dtype
    ang_r, ang_c = axial_angles(n)
    q = axial_rope(rms_norm_f32(q.reshape(b, n, H_A, HEAD_DIM), qn), ang_r, ang_c).astype(dt)
    k = axial_rope(rms_norm_f32(k.reshape(b, n, KV_A, HEAD_DIM), kn), ang_r, ang_c).astype(dt)
    v = v.reshape(b, n, KV_A, HEAD_DIM)
    nb = n // Q_BLOCK
    qblocks = jnp.moveaxis(q.reshape(b, nb, Q_BLOCK, KV_A, G_A, HEAD_DIM), 1, 0)
    scale = 1.0 / math.sqrt(HEAD_DIM)

    def block(qb):
        s = jnp.einsum('bqkgd,bskd->bkgqs', qb, k).astype(jnp.float32) * scale
        p = jax.nn.softmax(s, axis=-1).astype(v.dtype)
        return jnp.einsum('bkgqs,bskd->bqkgd', p, v)

    o = lax.map(block, qblocks)
    return jnp.moveaxis(o, 0, 1).reshape(b, n, W_A)


def t5_buckets(rel):
    nb = REL_BUCKETS // 2
    max_exact = nb // 2
    ret = jnp.where(rel > 0, nb, 0)
    n = jnp.abs(rel)
    nf = jnp.maximum(n, 1).astype(jnp.float32)
    large = max_exact + (jnp.log(nf / max_exact) / math.log(REL_MAX_DIST / max_exact) * (nb - max_exact)).astype(jnp.int32)
    large = jnp.minimum(large, nb - 1)
    return ret + jnp.where(n < max_exact, n, large)


def window_sink_attention(q, k, v, qn, kn, sink, rel_bias):
    b, n, _ = q.shape
    dt = q.dtype
    nb = n // Q_BLOCK
    q = rms_norm_f32(q.reshape(b, n, H_B, HEAD_DIM), qn).astype(dt)
    k = rms_norm_f32(k.reshape(b, n, KV_B, HEAD_DIM), kn).astype(dt)
    v = v.reshape(b, n, KV_B, HEAD_DIM)
    pad = ((0, 0), (Q_BLOCK, Q_BLOCK), (0, 0), (0, 0))
    kp = jnp.pad(k, pad).reshape(b, nb + 2, Q_BLOCK, KV_B, HEAD_DIM)
    vp = jnp.pad(v, pad).reshape(b, nb + 2, Q_BLOCK, KV_B, HEAD_DIM)
    kb = jnp.concatenate([kp[:, :-2], kp[:, 1:-1], kp[:, 2:]], axis=2)
    vb = jnp.concatenate([vp[:, :-2], vp[:, 1:-1], vp[:, 2:]], axis=2)
    qb = q.reshape(b, nb, Q_BLOCK, KV_B, G_B, HEAD_DIM)
    a = jnp.arange(Q_BLOCK)[:, None]
    j = jnp.arange(3 * Q_BLOCK)[None, :]
    rel = j - Q_BLOCK - a
    bias = jnp.transpose(rel_bias[t5_buckets(rel)], (2, 0, 1)).astype(jnp.float32)
    bias = bias.reshape(KV_B, G_B, Q_BLOCK, 3 * Q_BLOCK)
    kpos = (jnp.arange(nb) * Q_BLOCK - Q_BLOCK)[:, None, None] + j[None]
    valid = (jnp.abs(rel)[None] <= WINDOW) & (kpos >= 0) & (kpos < n)
    scale = 1.0 / math.sqrt(HEAD_DIM)
    s = jnp.einsum('bnqkgd,bnskd->bnkgqs', qb, kb).astype(jnp.float32) * scale + bias[None, None]
    s = jnp.where(valid[None, :, None, None], s, NEG_INF)
    snk = sink.astype(jnp.float32).reshape(1, 1, KV_B, G_B, 1, 1)
    m = jnp.maximum(jnp.max(s, axis=-1, keepdims=True), snk)
    e = jnp.exp(s - m)
    p = e / (jnp.sum(e, axis=-1, keepdims=True) + jnp.exp(snk - m))
    o = jnp.einsum('bnkgqs,bnskd->bnqkgd', p.astype(v.dtype), vb)
    return o.reshape(b, n, W_B)


def expert_choice_ffn(h, w_router, w_gate, w_up, w_down):
    b, n, d = h.shape
    xf = h.reshape(b * n, d)
    n_tok = b * n
    cap = EC_FACTOR * n_tok // N_EXPERTS
    aff = jax.nn.softmax(jnp.dot(xf, w_router).astype(jnp.float32), axis=-1)
    gates, idx = lax.top_k(aff.T, cap)
    xe = xf[idx]
    hid = jax.nn.silu(jnp.einsum('ecd,edf->ecf', xe, w_gate)) * jnp.einsum('ecd,edf->ecf', xe, w_up)
    out = jnp.einsum('ecf,efd->ecd', hid, w_down)
    contrib = (gates.astype(out.dtype)[..., None] * out).reshape(-1, d)
    y = jnp.zeros((n_tok, d), dtype=h.dtype).at[idx.reshape(-1)].add(contrib.astype(h.dtype))
    return y.reshape(b, n, d)


def encoder_layer(x, norm1, w_in, q_norm_a, k_norm_a, q_norm_b, k_norm_b, sink_b, rel_bias,
                  out_norm_a, out_norm_b, w_out, norm2, w_router, w_gate, w_up, w_down):
    h = rms_norm(x, norm1)
    proj = jnp.dot(h, w_in)
    cuts = [W_A, W_A + KVW_A, W_A + 2 * KVW_A, W_A + 2 * KVW_A + W_B, W_A + 2 * KVW_A + W_B + KVW_B]
    qa, ka, va, qb, kb, vb = jnp.split(proj, cuts, axis=-1)
    oa = global_axial_attention(qa, ka, va, q_norm_a, k_norm_a)
    ob = window_sink_attention(qb, kb, vb, q_norm_b, k_norm_b, sink_b, rel_bias)
    mixed = jnp.concatenate([rms_norm(oa, out_norm_a), rms_norm(ob, out_norm_b)], axis=-1)
    x = x + jnp.dot(mixed, w_out)
    x = x + expert_choice_ffn(rms_norm(x, norm2), w_router, w_gate, w_up, w_down)
    return x


def setup_inputs(seed: int = 0) -> dict:
    key = jax.random.key(seed)
    ks = jax.random.split(key, 20)
    f32 = jnp.float32
    nrm = lambda k, s, sc: jax.random.normal(k, s, f32) * sc
    gain = lambda k, s: 1.0 + 0.01 * jax.random.normal(k, s, f32)
    return {
        "x_prompt": jax.random.normal(ks[0], (BATCH, SEQ, D_MODEL), f32),
        "x_sample": jax.random.normal(ks[1], (DEC_BATCH, DEC_SEQ, D_MODEL), f32),
        "norm1": gain(ks[2], (DEPTH, D_MODEL)),
        "w_in": nrm(ks[3], (DEPTH, D_MODEL, IN_WIDTH), D_MODEL ** -0.5),
        "q_norm_a": gain(ks[4], (DEPTH, HEAD_DIM)),
        "k_norm_a": gain(ks[5], (DEPTH, HEAD_DIM)),
        "q_norm_b": gain(ks[6], (DEPTH, HEAD_DIM)),
        "k_norm_b": gain(ks[7], (DEPTH, HEAD_DIM)),
        "sink_b": nrm(ks[8], (DEPTH, H_B), 0.5),
        "rel_bias": nrm(ks[9], (REL_BUCKETS, H_B), 0.2),
        "out_norm_a": gain(ks[10], (DEPTH, W_A)),
        "out_norm_b": gain(ks[11], (DEPTH, W_B)),
        "w_out": nrm(ks[12], (DEPTH, MIX_WIDTH, D_MODEL), MIX_WIDTH ** -0.5),
        "norm2": gain(ks[13], (DEPTH, D_MODEL)),
        "w_router": nrm(ks[14], (DEPTH, D_MODEL, N_EXPERTS), D_MODEL ** -0.5),
        "w_gate": nrm(ks[15], (DEPTH, N_EXPERTS, D_MODEL, D_FF_EXPERT), D_MODEL ** -0.5),
        "w_up": nrm(ks[16], (DEPTH, N_EXPERTS, D_MODEL, D_FF_EXPERT), D_MODEL ** -0.5),
        "w_down": nrm(ks[17], (DEPTH, N_EXPERTS, D_FF_EXPERT, D_MODEL), D_FF_EXPERT ** -0.5),
    }


def reference(x_prompt, x_sample, norm1, w_in, q_norm_a, k_norm_a, q_norm_b, k_norm_b, sink_b, rel_bias,
              out_norm_a, out_norm_b, w_out, norm2, w_router, w_gate, w_up, w_down):
    def run(x):
        for l in range(DEPTH):
            x = encoder_layer(x, norm1[l], w_in[l], q_norm_a[l], k_norm_a[l], q_norm_b[l], k_norm_b[l],
                              sink_b[l], rel_bias, out_norm_a[l], out_norm_b[l], w_out[l], norm2[l],
                              w_router[l], w_gate[l], w_up[l], w_down[l])
        return x

    y_prompt = run(x_prompt)
    y_sample = run(x_sample)
    return (y_prompt, y_sample)
```

```python
import functools
import math

import jax
import jax.numpy as jnp
from jax import lax
from jax.experimental import pallas as pl
from jax.experimental.pallas import tpu as pltpu

F32 = jnp.float32
BF16 = jnp.bfloat16
I32 = jnp.int32
U32 = jnp.uint32

HEAD_DIM = 128
H_A = 16
KV_A = 4
H_B = 16
KV_B = 4
GROUP = 4
W_A = H_A * HEAD_DIM
W_B = H_B * HEAD_DIM
KVW = KV_A * HEAD_DIM
IN_WIDTH = W_A + 2 * KVW + W_B + 2 * KVW
GRID_W = 64
HALF_ROT = HEAD_DIM // 2
ROPE_THETA = 10000.0
REL_BUCKETS = 32
REL_MAX_DIST = 128
WINDOW = 128
N_EXPERTS = 16
EC_FACTOR = 2
NORM_EPS = 1e-6
NEG_INF = -1e30
SCALE = 1.0 / math.sqrt(HEAD_DIM)

V7X_VMEM_BYTES = 64 * 1024 * 1024
LANE = 128


def _vmem(mib):
    assert mib * 1024 * 1024 < V7X_VMEM_BYTES
    return mib * 1024 * 1024


def _params(sem, mib):
    return pltpu.CompilerParams(dimension_semantics=sem, vmem_limit_bytes=_vmem(mib))


IN_TN = 512
_ROPE_TILES = 5
_V_TILES = (5, 11)


def _inproj_kernel(x_ref, g1_ref, w_ref, gain_ref, cos_ref, sin_ref, o_ref, h_ref):
    j = pl.program_id(1)

    @pl.when(j == 0)
    def _():
        x = x_ref[...]
        ms = jnp.mean(x * x, axis=-1, keepdims=True)
        h_ref[...] = (x * lax.rsqrt(ms + NORM_EPS) * g1_ref[...]).astype(BF16)

    acc = jnp.dot(h_ref[...], w_ref[...], preferred_element_type=F32)
    is_v = jnp.logical_or(j == _V_TILES[0], j == _V_TILES[1])
    is_rope = j < _ROPE_TILES

    def head_norm():
        outs = []
        for hh in range(IN_TN // HEAD_DIM):
            a = acc[:, hh * HEAD_DIM:(hh + 1) * HEAD_DIM]
            ms = jnp.mean(a * a, axis=-1, keepdims=True)
            outs.append(a * lax.rsqrt(ms + NORM_EPS) * gain_ref[:, hh * HEAD_DIM:(hh + 1) * HEAD_DIM])
        return jnp.concatenate(outs, axis=-1)

    @pl.when(is_v)
    def _():
        o_ref[...] = acc.astype(BF16)

    @pl.when(jnp.logical_and(jnp.logical_not(is_v), jnp.logical_not(is_rope)))
    def _():
        o_ref[...] = head_norm().astype(BF16)

    @pl.when(is_rope)
    def _():
        y = head_norm()
        reps = IN_TN // HEAD_DIM
        c = jnp.concatenate([cos_ref[...]] * reps, axis=-1)
        s = jnp.concatenate([sin_ref[...]] * reps, axis=-1)
        lane = lax.broadcasted_iota(I32, y.shape, 1)
        first_half = (lane % (2 * (HALF_ROT // 2))) < (HALF_ROT // 2)
        partner = jnp.where(first_half,
                            pltpu.roll(y, IN_TN - HALF_ROT // 2, axis=1),
                            pltpu.roll(y, HALF_ROT // 2, axis=1))
        o_ref[...] = (y * c + partner * s).astype(BF16)


def _rope_tables(n):
    rows = n // GRID_W
    row = jnp.repeat(jnp.arange(rows, dtype=F32), GRID_W)
    col = jnp.tile(jnp.arange(GRID_W, dtype=F32), rows)
    inv = ROPE_THETA ** (-jnp.arange(0, HALF_ROT, 2, dtype=F32) / HALF_ROT)
    ar, ac = row[:, None] * inv, col[:, None] * inv
    cr, sr, cc, sc = jnp.cos(ar), jnp.sin(ar), jnp.cos(ac), jnp.sin(ac)
    cos_t = jnp.concatenate([cr, cr, cc, cc], axis=-1)
    sin_t = jnp.concatenate([-sr, sr, -sc, sc], axis=-1)
    return cos_t, sin_t


def _in_projection(x2, norm1, w_in_bf, gains, cos_t, sin_t, n_seq, tm):
    n_tok, d = x2.shape
    seq_blocks = n_seq // tm
    return pl.pallas_call(
        _inproj_kernel,
        grid=(n_tok // tm, IN_WIDTH // IN_TN),
        in_specs=[
            pl.BlockSpec((tm, d), lambda i, j: (i, 0)),
            pl.BlockSpec((1, d), lambda i, j: (0, 0)),
            pl.BlockSpec((d, IN_TN), lambda i, j: (0, j)),
            pl.BlockSpec((1, IN_TN), lambda i, j: (0, j)),
            pl.BlockSpec((tm, HEAD_DIM), lambda i, j: (i % seq_blocks, 0)),
            pl.BlockSpec((tm, HEAD_DIM), lambda i, j: (i % seq_blocks, 0)),
        ],
        out_specs=pl.BlockSpec((tm, IN_TN), lambda i, j: (i, j)),
        out_shape=jax.ShapeDtypeStruct((n_tok, IN_WIDTH), BF16),
        scratch_shapes=[pltpu.VMEM((tm, d), BF16)],
        compiler_params=_params(("parallel", "arbitrary"), 48),
        name="in_projection",
    )(x2, norm1.reshape(1, d), w_in_bf, gains, cos_t, sin_t)


def _flash_kernel(q_ref, k_ref, v_ref, o_ref, m_ref, l_ref, acc_ref):
    ki = pl.program_id(3)

    @pl.when(ki == 0)
    def _():
        m_ref[...] = jnp.full(m_ref.shape, -jnp.inf, F32)
        l_ref[...] = jnp.zeros(l_ref.shape, F32)
        acc_ref[...] = jnp.zeros(acc_ref.shape, F32)

    k = k_ref[...]
    v = v_ref[...]
    for g in range(GROUP):
        q = q_ref[:, g * HEAD_DIM:(g + 1) * HEAD_DIM]
        s = lax.dot_general(q, k, (((1,), (1,)), ((), ())), preferred_element_type=F32) * SCALE
        m_prev = m_ref[g]
        m_new = jnp.maximum(m_prev, jnp.max(s, axis=-1, keepdims=True))
        alpha = jnp.exp(m_prev - m_new)
        p = jnp.exp(s - m_new)
        l_ref[g] = alpha * l_ref[g] + jnp.sum(p, axis=-1, keepdims=True)
        acc_ref[g] = alpha * acc_ref[g] + jnp.dot(p.astype(BF16), v, preferred_element_type=F32)
        m_ref[g] = m_new

    @pl.when(ki == pl.num_programs(3) - 1)
    def _():
        for g in range(GROUP):
            o_ref[:, g * HEAD_DIM:(g + 1) * HEAD_DIM] = acc_ref[g] / l_ref[g]


def _attention_a(proj, b, n, tq, tk):
    n_tok = b * n
    qb, kb = n // tq, n // tk
    k_col = W_A // HEAD_DIM
    v_col = (W_A + KVW) // HEAD_DIM
    return pl.pallas_call(
        _flash_kernel,
        grid=(b, KV_A, qb, kb),
        in_specs=[
            pl.BlockSpec((tq, GROUP * HEAD_DIM), lambda bi, h, qi, ki: (bi * qb + qi, h)),
            pl.BlockSpec((tk, HEAD_DIM), lambda bi, h, qi, ki: (bi * kb + ki, k_col + h)),
            pl.BlockSpec((tk, HEAD_DIM), lambda bi, h, qi, ki: (bi * kb + ki, v_col + h)),
        ],
        out_specs=pl.BlockSpec((tq, GROUP * HEAD_DIM), lambda bi, h, qi, ki: (bi * qb + qi, h)),
        out_shape=jax.ShapeDtypeStruct((n_tok, W_A), F32),
        scratch_shapes=[pltpu.VMEM((GROUP, tq, 1), F32), pltpu.VMEM((GROUP, tq, 1), F32),
                        pltpu.VMEM((GROUP, tq, HEAD_DIM), F32)],
        compiler_params=_params(("parallel", "parallel", "parallel", "arbitrary"), 48),
        name="attention_global",
    )(proj, proj, proj)


WIN_TQ = 2 * WINDOW
WIN_TK = WIN_TQ + 2 * WINDOW


def _t5_buckets(rel):
    nb = REL_BUCKETS // 2
    max_exact = nb // 2
    ret = jnp.where(rel > 0, nb, 0)
    n = jnp.abs(rel)
    nf = jnp.maximum(n, 1).astype(F32)
    large = max_exact + (jnp.log(nf / max_exact) / math.log(REL_MAX_DIST / max_exact)
                         * (nb - max_exact)).astype(I32)
    large = jnp.minimum(large, nb - 1)
    return ret + jnp.where(n < max_exact, n, large)


def _window_bias(rel_bias):
    a = jnp.arange(WIN_TQ)[:, None]
    j = jnp.arange(WIN_TK)[None, :]
    rel = j - WINDOW - a
    bias = jnp.transpose(rel_bias[_t5_buckets(rel)], (2, 0, 1)).astype(F32)
    return jnp.where((jnp.abs(rel) <= WINDOW)[None], bias, NEG_INF)


def _window_kernel(sink_ref, q_ref, kp_ref, kc_ref, kn_ref, vp_ref, vc_ref, vn_ref, bias_ref, o_ref, *, n):
    h = pl.program_id(0)
    i = pl.program_id(2)
    kcat = jnp.concatenate([kp_ref[...], kc_ref[...], kn_ref[...]], axis=0)
    vcat = jnp.concatenate([vp_ref[...], vc_ref[...], vn_ref[...]], axis=0)
    kpos = i * WIN_TQ - WINDOW + lax.broadcasted_iota(I32, (WIN_TQ, WIN_TK), 1)
    valid = jnp.logical_and(kpos >= 0, kpos < n)
    for g in range(GROUP):
        q = q_ref[:, g * HEAD_DIM:(g + 1) * HEAD_DIM]
        s = lax.dot_general(q, kcat, (((1,), (1,)), ((), ())), preferred_element_type=F32) * SCALE
        s = jnp.where(valid, s + bias_ref[g], NEG_INF)
        snk = sink_ref[h * GROUP + g]
        m = jnp.maximum(jnp.max(s, axis=-1, keepdims=True), snk)
        e = jnp.exp(s - m)
        p = e / (jnp.sum(e, axis=-1, keepdims=True) + jnp.exp(snk - m))
        o_ref[:, g * HEAD_DIM:(g + 1) * HEAD_DIM] = jnp.dot(p.astype(BF16), vcat, preferred_element_type=F32)


def _attention_b(proj, sink, bias, b, n):
    n_tok = b * n
    qb = n // WIN_TQ
    nb128 = n // WINDOW
    base = W_A + 2 * KVW
    q_col = base // (GROUP * HEAD_DIM)
    k_col = (base + W_B) // HEAD_DIM
    v_col = (base + W_B + KVW) // HEAD_DIM

    def prev_map(col):
        return lambda h, bi, i, s: (bi * nb128 + jnp.maximum(2 * i - 1, 0), col + h)

    def cur_map(col):
        return lambda h, bi, i, s: (bi * qb + i, col + h)

    def next_map(col):
        return lambda h, bi, i, s: (bi * nb128 + jnp.minimum(2 * i + 2, nb128 - 1), col + h)

    grid_spec = pltpu.PrefetchScalarGridSpec(
        num_scalar_prefetch=1,
        grid=(KV_B, b, qb),
        in_specs=[
            pl.BlockSpec((WIN_TQ, GROUP * HEAD_DIM), lambda h, bi, i, s: (bi * qb + i, q_col + h)),
            pl.BlockSpec((WINDOW, HEAD_DIM), prev_map(k_col)),
            pl.BlockSpec((WIN_TQ, HEAD_DIM), cur_map(k_col)),
            pl.BlockSpec((WINDOW, HEAD_DIM), next_map(k_col)),
            pl.BlockSpec((WINDOW, HEAD_DIM), prev_map(v_col)),
            pl.BlockSpec((WIN_TQ, HEAD_DIM), cur_map(v_col)),
            pl.BlockSpec((WINDOW, HEAD_DIM), next_map(v_col)),
            pl.BlockSpec((GROUP, WIN_TQ, WIN_TK), lambda h, bi, i, s: (h, 0, 0)),
        ],
        out_specs=pl.BlockSpec((WIN_TQ, GROUP * HEAD_DIM), lambda h, bi, i, s: (bi * qb + i, h)),
    )
    return pl.pallas_call(
        functools.partial(_window_kernel, n=n),
        grid_spec=grid_spec,
        out_shape=jax.ShapeDtypeStruct((n_tok, W_B), F32),
        compiler_params=_params(("parallel", "parallel", "parallel"), 32),
        name="attention_window",
    )(sink, proj, proj, proj, proj, proj, proj, proj, bias)


def _outproj_kernel(oa_ref, ob_ref, ga_ref, gb_ref, w_ref, x_ref, o_ref, mix_ref):
    j = pl.program_id(1)

    @pl.when(j == 0)
    def _():
        for src, gain, lo in ((oa_ref, ga_ref, 0), (ob_ref, gb_ref, W_A)):
            o = src[...]
            ms = jnp.mean(o * o, axis=-1, keepdims=True)
            mix_ref[:, lo:lo + o.shape[1]] = (o * lax.rsqrt(ms + NORM_EPS) * gain[...]).astype(BF16)

    o_ref[...] = x_ref[...] + jnp.dot(mix_ref[...], w_ref[...], preferred_element_type=F32)


def _out_projection(oa, ob, gain_a, gain_b, w_out_bf, x2, tm, tn):
    n_tok, d = x2.shape
    return pl.pallas_call(
        _outproj_kernel,
        grid=(n_tok // tm, d // tn),
        in_specs=[
            pl.BlockSpec((tm, W_A), lambda i, j: (i, 0)),
            pl.BlockSpec((tm, W_B), lambda i, j: (i, 0)),
            pl.BlockSpec((1, W_A), lambda i, j: (0, 0)),
            pl.BlockSpec((1, W_B), lambda i, j: (0, 0)),
            pl.BlockSpec((W_A + W_B, tn), lambda i, j: (0, j)),
            pl.BlockSpec((tm, tn), lambda i, j: (i, j)),
        ],
        out_specs=pl.BlockSpec((tm, tn), lambda i, j: (i, j)),
        out_shape=jax.ShapeDtypeStruct((n_tok, d), F32),
        scratch_shapes=[pltpu.VMEM((tm, W_A + W_B), BF16)],
        compiler_params=_params(("parallel", "arbitrary"), 48),
        name="out_projection",
    )(oa, ob, gain_a.reshape(1, W_A), gain_b.reshape(1, W_B), w_out_bf, x2)


def _router_kernel(x_ref, g_ref, wr_ref, hp_ref, aff_ref):
    x = x_ref[...]
    ms = jnp.mean(x * x, axis=-1, keepdims=True)
    hb = (x * lax.rsqrt(ms + NORM_EPS) * g_ref[...]).astype(BF16)
    logits = jnp.dot(hb, wr_ref[...], preferred_element_type=F32)
    e = jnp.exp(logits - jnp.max(logits, axis=-1, keepdims=True))
    aff_ref[...] = e / jnp.sum(e, axis=-1, keepdims=True)
    half = hb.shape[1] // 2
    bits = pltpu.bitcast(hb.astype(F32), U32)
    hp_ref[...] = (bits[:, :half] >> 16) | (bits[:, half:] & jnp.uint32(0xFFFF0000))


def _router(x1, norm2, w_router_bf, tm):
    n_tok, d = x1.shape
    return pl.pallas_call(
        _router_kernel,
        grid=(n_tok // tm,),
        in_specs=[
            pl.BlockSpec((tm, d), lambda i: (i, 0)),
            pl.BlockSpec((1, d), lambda i: (0, 0)),
            pl.BlockSpec((d, N_EXPERTS), lambda i: (0, 0)),
        ],
        out_specs=[pl.BlockSpec((tm, d // 2), lambda i: (i, 0)),
                   pl.BlockSpec((tm, N_EXPERTS), lambda i: (i, 0))],
        out_shape=[jax.ShapeDtypeStruct((n_tok, d // 2), U32),
                   jax.ShapeDtypeStruct((n_tok, N_EXPERTS), F32)],
        compiler_params=_params(("parallel",), 48),
        name="router",
    )(x1, norm2.reshape(1, d), w_router_bf)


SEL_T = 256
SEL_CHUNK = 1024


def _select_kernel(aff_ref, idx_ref, gate_ref, off_ref, thr_ref, need_ref, carry_ref, *, cap, n_tok):
    t = pl.program_id(0)
    n_chunks = n_tok // SEL_CHUNK

    def count(pred_fn):
        def body(c, tot):
            bits = pltpu.bitcast(aff_ref[pl.ds(c * SEL_CHUNK, SEL_CHUNK), :], I32)
            return tot + jnp.sum(pred_fn(bits).astype(I32), axis=0, keepdims=True)
        return lax.fori_loop(0, n_chunks, body, jnp.zeros((1, N_EXPERTS), I32))

    @pl.when(t == 0)
    def _():
        def bit_body(b, thr):
            cand = thr | jnp.left_shift(jnp.int32(1), 30 - b)
            return jnp.where(count(lambda bits: bits >= cand) >= cap, cand, thr)
        thr = lax.fori_loop(0, 31, bit_body, jnp.zeros((1, N_EXPERTS), I32))
        thr_ref[...] = thr
        need_ref[...] = (cap - count(lambda bits: bits > thr)).astype(F32)
        carry_ref[...] = jnp.zeros(carry_ref.shape, F32)
        idx_ref[...] = jnp.zeros(idx_ref.shape, I32)
        gate_ref[...] = jnp.zeros(gate_ref.shape, F32)

    a = aff_ref[pl.ds(t * SEL_T, SEL_T), :]
    bits = pltpu.bitcast(a, I32)
    thr = thr_ref[...]
    gt = bits > thr
    eq = bits == thr
    r = lax.broadcasted_iota(I32, (SEL_T, SEL_T), 0)
    c = lax.broadcasted_iota(I32, (SEL_T, SEL_T), 1)
    tri = jnp.where(r >= c, 1.0, 0.0).astype(BF16)
    eq_f = jnp.where(eq, 1.0, 0.0)
    cum_eq = jnp.dot(tri, eq_f.astype(BF16), preferred_element_type=F32) + carry_ref[0:1, :]
    sel = jnp.logical_or(gt, jnp.logical_and(eq, cum_eq <= need_ref[...]))
    sel_f = jnp.where(sel, 1.0, 0.0)
    pos = jnp.dot(tri, sel_f.astype(BF16), preferred_element_type=F32) + carry_ref[1:2, :]
    off_ref[0] = carry_ref[1:2, :].astype(I32)
    carry_ref[0:1, :] = carry_ref[0:1, :] + jnp.sum(eq_f, axis=0, keepdims=True)
    carry_ref[1:2, :] = carry_ref[1:2, :] + jnp.sum(sel_f, axis=0, keepdims=True)
    slot = jnp.where(sel, pos - 1.0, -1.0).astype(I32)
    tok = t * SEL_T + lax.broadcasted_iota(I32, (SEL_T, 1), 0)
    lane = lax.broadcasted_iota(I32, (1, cap), 1)
    for e in range(N_EXPERTS):
        match = slot[:, e:e + 1] == lane
        idx_ref[e:e + 1, :] += jnp.sum(jnp.where(match, tok, 0), axis=0, keepdims=True)
        gate_ref[e:e + 1, :] += jnp.sum(jnp.where(match, a[:, e:e + 1], 0.0), axis=0, keepdims=True)


def _select(aff, cap):
    n_tok = aff.shape[0]
    nt = n_tok // SEL_T
    return pl.pallas_call(
        functools.partial(_select_kernel, cap=cap, n_tok=n_tok),
        grid=(nt,),
        in_specs=[pl.BlockSpec((n_tok, N_EXPERTS), lambda t: (0, 0))],
        out_specs=[pl.BlockSpec((N_EXPERTS, cap), lambda t: (0, 0)),
                   pl.BlockSpec((N_EXPERTS, cap), lambda t: (0, 0)),
                   pl.BlockSpec((1, 1, N_EXPERTS), lambda t: (t, 0, 0))],
        out_shape=[jax.ShapeDtypeStruct((N_EXPERTS, cap), I32),
                   jax.ShapeDtypeStruct((N_EXPERTS, cap), F32),
                   jax.ShapeDtypeStruct((nt, 1, N_EXPERTS), I32)],
        scratch_shapes=[pltpu.VMEM((1, N_EXPERTS), I32), pltpu.VMEM((1, N_EXPERTS), F32),
                        pltpu.VMEM((2, N_EXPERTS), F32)],
        compiler_params=_params(("arbitrary",), 48),
        name="expert_select",
    )(aff)


def _gather_kernel(idx_ref, hp_ref, xe_ref, sem, *, rows):
    def row_copy(s):
        return pltpu.make_async_copy(hp_ref.at[pl.ds(idx_ref[0, 0, s], 1)], xe_ref.at[0, pl.ds(s, 1)], sem)

    def start(s, carry):
        row_copy(s).start()
        return carry

    def wait(s, carry):
        row_copy(s).wait()
        return carry

    lax.fori_loop(0, rows, start, 0)
    lax.fori_loop(0, rows, wait, 0)


def _gather(idx, hp, cap, rows):
    half = hp.shape[1]
    n_blocks = cap // rows
    idx3 = idx.reshape(N_EXPERTS * n_blocks, 1, rows)
    return pl.pallas_call(
        functools.partial(_gather_kernel, rows=rows),
        grid=(N_EXPERTS, n_blocks),
        in_specs=[pl.BlockSpec((1, 1, rows), lambda e, r: (e * n_blocks + r, 0, 0), memory_space=pltpu.SMEM),
                  pl.BlockSpec(memory_space=pl.ANY)],
        out_specs=pl.BlockSpec((1, rows, half), lambda e, r: (e, r, 0)),
        out_shape=jax.ShapeDtypeStruct((N_EXPERTS, cap, half), U32),
        scratch_shapes=[pltpu.SemaphoreType.DMA(())],
        compiler_params=_params(("arbitrary", "arbitrary"), 32),
        name="expert_gather",
    )(idx3, hp)


def _unpack(xp):
    lo = pltpu.bitcast(xp << 16, F32).astype(BF16)
    hi = pltpu.bitcast(xp & jnp.uint32(0xFFFF0000), F32).astype(BF16)
    return lo, hi


def _ffn_up_kernel(xe_ref, wg_ref, wu_ref, h_ref):
    lo, hi = _unpack(xe_ref[0])
    half = lo.shape[1]

    def proj(w_ref):
        return (jnp.dot(lo, w_ref[0, :half, :], preferred_element_type=F32)
                + jnp.dot(hi, w_ref[0, half:, :], preferred_element_type=F32))

    gate = proj(wg_ref)
    up = proj(wu_ref)
    h_ref[0] = (gate * jax.nn.sigmoid(gate) * up).astype(BF16)


def _ffn_up(xe, wg_bf, wu_bf, tr, tf):
    _, cap, half = xe.shape
    d, f = wg_bf.shape[1], wg_bf.shape[2]
    return pl.pallas_call(
        _ffn_up_kernel,
        grid=(N_EXPERTS, f // tf, cap // tr),
        in_specs=[pl.BlockSpec((1, tr, half), lambda e, j, r: (e, r, 0)),
                  pl.BlockSpec((1, d, tf), lambda e, j, r: (e, 0, j)),
                  pl.BlockSpec((1, d, tf), lambda e, j, r: (e, 0, j))],
        out_specs=pl.BlockSpec((1, tr, tf), lambda e, j, r: (e, r, j)),
        out_shape=jax.ShapeDtypeStruct((N_EXPERTS, cap, f), BF16),
        compiler_params=_params(("parallel", "parallel", "arbitrary"), 56),
        name="expert_ffn_up",
    )(xe, wg_bf, wu_bf)


def _ffn_down_kernel(h_ref, wd_ref, g_ref, o_ref):
    out = jnp.dot(h_ref[0], wd_ref[0], preferred_element_type=F32)
    o_ref[0] = (g_ref[0] * out).astype(BF16)


def _ffn_down(hid, wd_bf, gates, tr, tn):
    _, cap, f = hid.shape
    d = wd_bf.shape[2]
    return pl.pallas_call(
        _ffn_down_kernel,
        grid=(N_EXPERTS, d // tn, cap // tr),
        in_specs=[pl.BlockSpec((1, tr, f), lambda e, j, r: (e, r, 0)),
                  pl.BlockSpec((1, f, tn), lambda e, j, r: (e, 0, j)),
                  pl.BlockSpec((1, tr, 1), lambda e, j, r: (e, r, 0))],
        out_specs=pl.BlockSpec((1, tr, tn), lambda e, j, r: (e, r, j)),
        out_shape=jax.ShapeDtypeStruct((N_EXPERTS, cap, d), BF16),
        compiler_params=_params(("parallel", "parallel", "arbitrary"), 48),
        name="expert_ffn_down",
    )(hid, wd_bf, gates.reshape(N_EXPERTS, cap, 1))


def _combine_kernel(blk0_ref, blk1_ref, cnt_ref, x_ref, idx_ref, con_ref, y_ref, *, nt):
    t = pl.program_id(0)
    e = pl.program_id(1)
    c = pl.program_id(2)

    @pl.when(jnp.logical_and(e == 0, c == 0))
    def _():
        y_ref[...] = x_ref[...]

    k = e * nt + t
    needed = jnp.logical_and(cnt_ref[k] > 0, jnp.logical_or(c == 0, blk1_ref[k] != blk0_ref[k]))

    @pl.when(needed)
    def _():
        tok = t * SEL_T + lax.broadcasted_iota(I32, (SEL_T, 1), 0)
        onehot = jnp.where(idx_ref[0, 0] == tok, 1.0, 0.0).astype(BF16)
        y_ref[...] += jnp.dot(onehot, con_ref[0], preferred_element_type=F32)


def _combine(x1, idx, contrib, offs, cap):
    n_tok, d = x1.shape
    nt = n_tok // SEL_T
    r = SEL_T
    n_chunks = cap // r
    lo = jnp.transpose(offs.reshape(nt, N_EXPERTS))
    hi = jnp.concatenate([lo[:, 1:], jnp.full((N_EXPERTS, 1), cap, I32)], axis=1)
    cnt = (hi - lo).reshape(-1)
    blk0 = jnp.minimum(lo // r, n_chunks - 1).reshape(-1)
    blk1 = jnp.where(hi > lo, (hi - 1) // r, jnp.minimum(lo // r, n_chunks - 1)).reshape(-1)
    idx4 = idx.reshape(N_EXPERTS, n_chunks, 1, r)

    def chunk(t, e, c, b0, b1, cn):
        k = e * nt + t
        return jnp.where(c == 0, b0[k], b1[k])

    grid_spec = pltpu.PrefetchScalarGridSpec(
        num_scalar_prefetch=3,
        grid=(nt, N_EXPERTS, 2),
        in_specs=[
            pl.BlockSpec((SEL_T, d), lambda t, e, c, b0, b1, cn: (t, 0)),
            pl.BlockSpec((1, 1, 1, r), lambda t, e, c, b0, b1, cn: (e, chunk(t, e, c, b0, b1, cn), 0, 0)),
            pl.BlockSpec((1, r, d), lambda t, e, c, b0, b1, cn: (e, chunk(t, e, c, b0, b1, cn), 0)),
        ],
        out_specs=pl.BlockSpec((SEL_T, d), lambda t, e, c, b0, b1, cn: (t, 0)),
    )
    return pl.pallas_call(
        functools.partial(_combine_kernel, nt=nt),
        grid_spec=grid_spec,
        out_shape=jax.ShapeDtypeStruct((n_tok, d), F32),
        compiler_params=_params(("parallel", "arbitrary", "arbitrary"), 48),
        name="expert_combine",
    )(blk0, blk1, cnt, x1, idx4, contrib)


def _tiles(n_tok, n_seq, cap):
    return dict(
        tm=min(512, n_seq),
        tq=min(512, n_seq),
        tk=min(1024, n_seq),
        rows=min(512, cap),
        tr=min(1024, cap),
    )


def _encoder_layer(x, p):
    b, n, d = x.shape
    n_tok = b * n
    cap = EC_FACTOR * n_tok // N_EXPERTS
    tl = _tiles(n_tok, n, cap)
    x2 = x.reshape(n_tok, d)

    proj = _in_projection(x2, p["norm1"], p["w_in"], p["gains"], p["cos"][n], p["sin"][n], n, tl["tm"])
    oa = _attention_a(proj, b, n, tl["tq"], tl["tk"])
    ob = _attention_b(proj, p["sink"], p["bias"], b, n)
    x1 = _out_projection(oa, ob, p["out_norm_a"], p["out_norm_b"], p["w_out"], x2, tl["tm"], 512)

    hp, aff = _router(x1, p["norm2"], p["w_router"], tl["tm"])
    idx, gates, offs = _select(aff, cap)
    xe = _gather(idx, hp, cap, tl["rows"])
    hid = _ffn_up(xe, p["w_gate"], p["w_up"], tl["tr"], 512)
    contrib = _ffn_down(hid, p["w_down"], gates, tl["tr"], 1024)
    y = _combine(x1, idx, contrib, offs, cap)
    return y.reshape(b, n, d)


def kernel(x_prompt, x_sample, norm1, w_in, q_norm_a, k_norm_a, q_norm_b, k_norm_b, sink_b, rel_bias,
           out_norm_a, out_norm_b, w_out, norm2, w_router, w_gate, w_up, w_down):
    depth = norm1.shape[0]
    seqs = sorted({x_prompt.shape[1], x_sample.shape[1]})
    tables = {n: _rope_tables(n) for n in seqs}
    bias = _window_bias(rel_bias)
    ones = jnp.ones((KVW,), F32)

    layers = []
    for l in range(depth):
        gains = jnp.concatenate([jnp.tile(q_norm_a[l], H_A), jnp.tile(k_norm_a[l], KV_A), ones,
                                 jnp.tile(q_norm_b[l], H_B), jnp.tile(k_norm_b[l], KV_B), ones])
        layers.append(dict(
            norm1=norm1[l], w_in=w_in[l].astype(BF16), gains=gains.reshape(1, IN_WIDTH),
            cos={n: tables[n][0] for n in seqs}, sin={n: tables[n][1] for n in seqs},
            sink=sink_b[l], bias=bias,
            out_norm_a=out_norm_a[l], out_norm_b=out_norm_b[l], w_out=w_out[l].astype(BF16),
            norm2=norm2[l], w_router=w_router[l].astype(BF16),
            w_gate=w_gate[l].astype(BF16), w_up=w_up[l].astype(BF16), w_down=w_down[l].astype(BF16),
        ))

    def run(x):
        for p in layers:
            x = _encoder_layer(x, p)
        return x

    return (run(x_prompt), run(x_sample))
```

```python
import functools
import math

import jax
import jax.numpy as jnp
from jax import lax
from jax.experimental import pallas as pl
from jax.experimental.pallas import tpu as pltpu

F32 = jnp.float32
BF16 = jnp.bfloat16
I32 = jnp.int32
U32 = jnp.uint32

HEAD_DIM = 128
H_A = 16
KV_A = 4
H_B = 16
KV_B = 4
GROUP = 4
W_A = H_A * HEAD_DIM
W_B = H_B * HEAD_DIM
KVW = KV_A * HEAD_DIM
IN_WIDTH = W_A + 2 * KVW + W_B + 2 * KVW
GRID_W = 64
HALF_ROT = HEAD_DIM // 2
ROPE_THETA = 10000.0
REL_BUCKETS = 32
REL_MAX_DIST = 128
WINDOW = 128
N_EXPERTS = 16
EC_FACTOR = 2
NORM_EPS = 1e-6
NEG_INF = -1e30
SCALE = 1.0 / math.sqrt(HEAD_DIM)

V7X_VMEM_BYTES = 64 * 1024 * 1024
LANE = 128


def _vmem(mib):
    assert mib * 1024 * 1024 < V7X_VMEM_BYTES
    return mib * 1024 * 1024


def _params(sem, mib):
    return pltpu.CompilerParams(dimension_semantics=sem, vmem_limit_bytes=_vmem(mib))


IN_TN = 512
_ROPE_TILES = 5
_V_TILES = (5, 11)


def _inproj_kernel(x_ref, g1_ref, w_ref, gain_ref, cos_ref, sin_ref, o_ref, h_ref):
    j = pl.program_id(1)

    @pl.when(j == 0)
    def _():
        x = x_ref[...]
        ms = jnp.mean(x * x, axis=-1, keepdims=True)
        h_ref[...] = (x * lax.rsqrt(ms + NORM_EPS) * g1_ref[...]).astype(BF16)

    acc = jnp.dot(h_ref[...], w_ref[...], preferred_element_type=F32)
    is_v = jnp.logical_or(j == _V_TILES[0], j == _V_TILES[1])
    is_rope = j < _ROPE_TILES

    def head_norm():
        outs = []
        for hh in range(IN_TN // HEAD_DIM):
            a = acc[:, hh * HEAD_DIM:(hh + 1) * HEAD_DIM]
            ms = jnp.mean(a * a, axis=-1, keepdims=True)
            outs.append(a * lax.rsqrt(ms + NORM_EPS) * gain_ref[:, hh * HEAD_DIM:(hh + 1) * HEAD_DIM])
        return jnp.concatenate(outs, axis=-1)

    @pl.when(is_v)
    def _():
        o_ref[...] = acc.astype(BF16)

    @pl.when(jnp.logical_and(jnp.logical_not(is_v), jnp.logical_not(is_rope)))
    def _():
        o_ref[...] = head_norm().astype(BF16)

    @pl.when(is_rope)
    def _():
        y = head_norm()
        reps = IN_TN // HEAD_DIM
        c = jnp.concatenate([cos_ref[...]] * reps, axis=-1)
        s = jnp.concatenate([sin_ref[...]] * reps, axis=-1)
        lane = lax.broadcasted_iota(I32, y.shape, 1)
        first_half = (lane % (2 * (HALF_ROT // 2))) < (HALF_ROT // 2)
        partner = jnp.where(first_half,
                            pltpu.roll(y, IN_TN - HALF_ROT // 2, axis=1),
                            pltpu.roll(y, HALF_ROT // 2, axis=1))
        qscale = jnp.where(j < W_A // IN_TN, SCALE * LOG2E, 1.0)
        o_ref[...] = ((y * c + partner * s) * qscale).astype(BF16)


def _rope_tables(n):
    rows = n // GRID_W
    row = jnp.repeat(jnp.arange(rows, dtype=F32), GRID_W)
    col = jnp.tile(jnp.arange(GRID_W, dtype=F32), rows)
    inv = ROPE_THETA ** (-jnp.arange(0, HALF_ROT, 2, dtype=F32) / HALF_ROT)
    ar, ac = row[:, None] * inv, col[:, None] * inv
    cr, sr, cc, sc = jnp.cos(ar), jnp.sin(ar), jnp.cos(ac), jnp.sin(ac)
    cos_t = jnp.concatenate([cr, cr, cc, cc], axis=-1)
    sin_t = jnp.concatenate([-sr, sr, -sc, sc], axis=-1)
    return cos_t, sin_t


def _in_projection(x2, norm1, w_in_bf, gains, cos_t, sin_t, n_seq, tm):
    n_tok, d = x2.shape
    seq_blocks = n_seq // tm
    return pl.pallas_call(
        _inproj_kernel,
        grid=(n_tok // tm, IN_WIDTH // IN_TN),
        in_specs=[
            pl.BlockSpec((tm, d), lambda i, j: (i, 0)),
            pl.BlockSpec((1, d), lambda i, j: (0, 0)),
            pl.BlockSpec((d, IN_TN), lambda i, j: (0, j)),
            pl.BlockSpec((1, IN_TN), lambda i, j: (0, j)),
            pl.BlockSpec((tm, HEAD_DIM), lambda i, j: (i % seq_blocks, 0)),
            pl.BlockSpec((tm, HEAD_DIM), lambda i, j: (i % seq_blocks, 0)),
        ],
        out_specs=pl.BlockSpec((tm, IN_TN), lambda i, j: (i, j)),
        out_shape=jax.ShapeDtypeStruct((n_tok, IN_WIDTH), BF16),
        scratch_shapes=[pltpu.VMEM((tm, d), BF16)],
        compiler_params=_params(("parallel", "arbitrary"), 48),
        name="in_projection",
    )(x2, norm1.reshape(1, d), w_in_bf, gains, cos_t, sin_t)


ATT_TQ = 128
ATT_TKC = 512
ATT_UNROLL = 8
LOG2E = 1.4426950408889634


def _flash_kernel(q_ref, k_ref, v_ref, o_ref, *, n_chunks, unroll):
    q = q_ref[...]
    tq = q.shape[0]
    q_all = jnp.concatenate([q[:, g * HEAD_DIM:(g + 1) * HEAD_DIM] for g in range(GROUP)], axis=0)
    nq = GROUP * tq

    def body(c, carry):
        m, l, acc = carry
        starts = [pl.multiple_of((c * unroll + u) * ATT_TKC, ATT_TKC) for u in range(unroll)]
        scores = [lax.dot_general(k_ref[pl.ds(st, ATT_TKC), :], q_all, (((1,), (1,)), ((), ())),
                                  preferred_element_type=F32) for st in starts]
        for st, s in zip(starts, scores):
            m_new = jnp.maximum(m, jnp.max(s, axis=0, keepdims=True))
            alpha = jnp.exp2(m - m_new)
            p = jnp.exp2(s - m_new)
            l = alpha * l + jnp.sum(p, axis=0, keepdims=True)
            pv = lax.dot_general(v_ref[pl.ds(st, ATT_TKC), :], p.astype(BF16), (((0,), (0,)), ((), ())),
                                 preferred_element_type=F32)
            acc = alpha * acc + pv
            m = m_new
        return m, l, acc

    init = (jnp.full((1, nq), -jnp.inf, F32), jnp.zeros((1, nq), F32), jnp.zeros((HEAD_DIM, nq), F32))
    _, l, acc = lax.fori_loop(0, n_chunks // unroll, body, init)
    o_t = acc / l
    for g in range(GROUP):
        o_ref[:, g * HEAD_DIM:(g + 1) * HEAD_DIM] = o_t[:, g * tq:(g + 1) * tq].T


def _attention_a(proj, b, n, tq):
    n_tok = b * n
    qb = n // tq
    k_col = W_A // HEAD_DIM
    v_col = (W_A + KVW) // HEAD_DIM
    n_chunks = n // ATT_TKC
    unroll = math.gcd(n_chunks, ATT_UNROLL)
    return pl.pallas_call(
        functools.partial(_flash_kernel, n_chunks=n_chunks, unroll=unroll),
        grid=(b, KV_A, qb),
        in_specs=[
            pl.BlockSpec((tq, GROUP * HEAD_DIM), lambda bi, h, qi: (bi * qb + qi, h)),
            pl.BlockSpec((n, HEAD_DIM), lambda bi, h, qi: (bi, k_col + h)),
            pl.BlockSpec((n, HEAD_DIM), lambda bi, h, qi: (bi, v_col + h)),
        ],
        out_specs=pl.BlockSpec((tq, GROUP * HEAD_DIM), lambda bi, h, qi: (bi * qb + qi, h)),
        out_shape=jax.ShapeDtypeStruct((n_tok, W_A), F32),
        compiler_params=_params(("parallel", "parallel", "arbitrary"), 48),
        name="attention_global",
    )(proj, proj, proj)


WIN_TQ = 2 * WINDOW
WIN_TK = WIN_TQ + 2 * WINDOW


def _t5_buckets(rel):
    nb = REL_BUCKETS // 2
    max_exact = nb // 2
    ret = jnp.where(rel > 0, nb, 0)
    n = jnp.abs(rel)
    nf = jnp.maximum(n, 1).astype(F32)
    large = max_exact + (jnp.log(nf / max_exact) / math.log(REL_MAX_DIST / max_exact)
                         * (nb - max_exact)).astype(I32)
    large = jnp.minimum(large, nb - 1)
    return ret + jnp.where(n < max_exact, n, large)


def _window_bias(rel_bias):
    span = WIN_TQ + WIN_TK
    kk = jnp.arange(span)
    rel = jnp.where(kk < WIN_TK, kk, kk - span) - WINDOW
    tbl = jnp.where((jnp.abs(rel) <= WINDOW)[:, None], rel_bias[_t5_buckets(rel)].astype(F32), NEG_INF)
    flat = jnp.tile(tbl.T, (1, WIN_TQ))[:, :WIN_TQ * (span - 1)]
    return flat.reshape(H_B, WIN_TQ, span - 1)[:, :, :WIN_TK]


def _window_kernel(sink_ref, q_ref, kp_ref, kc_ref, kn_ref, vp_ref, vc_ref, vn_ref, bias_ref, o_ref, *, n):
    h = pl.program_id(0)
    i = pl.program_id(2)
    kcat = jnp.concatenate([kp_ref[...], kc_ref[...], kn_ref[...]], axis=0)
    vcat = jnp.concatenate([vp_ref[...], vc_ref[...], vn_ref[...]], axis=0)
    kpos = i * WIN_TQ - WINDOW + lax.broadcasted_iota(I32, (WIN_TQ, WIN_TK), 1)
    valid = jnp.logical_and(kpos >= 0, kpos < n)
    for g in range(GROUP):
        q = q_ref[:, g * HEAD_DIM:(g + 1) * HEAD_DIM]
        s = lax.dot_general(q, kcat, (((1,), (1,)), ((), ())), preferred_element_type=F32) * SCALE
        s = jnp.where(valid, s + bias_ref[g], NEG_INF)
        snk = sink_ref[h * GROUP + g]
        m = jnp.maximum(jnp.max(s, axis=-1, keepdims=True), snk)
        e = jnp.exp(s - m)
        p = e / (jnp.sum(e, axis=-1, keepdims=True) + jnp.exp(snk - m))
        o_ref[:, g * HEAD_DIM:(g + 1) * HEAD_DIM] = jnp.dot(p.astype(BF16), vcat, preferred_element_type=F32)


def _attention_b(proj, sink, bias, b, n):
    n_tok = b * n
    qb = n // WIN_TQ
    nb128 = n // WINDOW
    base = W_A + 2 * KVW
    q_col = base // (GROUP * HEAD_DIM)
    k_col = (base + W_B) // HEAD_DIM
    v_col = (base + W_B + KVW) // HEAD_DIM

    def prev_map(col):
        return lambda h, bi, i, s: (bi * nb128 + jnp.maximum(2 * i - 1, 0), col + h)

    def cur_map(col):
        return lambda h, bi, i, s: (bi * qb + i, col + h)

    def next_map(col):
        return lambda h, bi, i, s: (bi * nb128 + jnp.minimum(2 * i + 2, nb128 - 1), col + h)

    grid_spec = pltpu.PrefetchScalarGridSpec(
        num_scalar_prefetch=1,
        grid=(KV_B, b, qb),
        in_specs=[
            pl.BlockSpec((WIN_TQ, GROUP * HEAD_DIM), lambda h, bi, i, s: (bi * qb + i, q_col + h)),
            pl.BlockSpec((WINDOW, HEAD_DIM), prev_map(k_col)),
            pl.BlockSpec((WIN_TQ, HEAD_DIM), cur_map(k_col)),
            pl.BlockSpec((WINDOW, HEAD_DIM), next_map(k_col)),
            pl.BlockSpec((WINDOW, HEAD_DIM), prev_map(v_col)),
            pl.BlockSpec((WIN_TQ, HEAD_DIM), cur_map(v_col)),
            pl.BlockSpec((WINDOW, HEAD_DIM), next_map(v_col)),
            pl.BlockSpec((GROUP, WIN_TQ, WIN_TK), lambda h, bi, i, s: (h, 0, 0)),
        ],
        out_specs=pl.BlockSpec((WIN_TQ, GROUP * HEAD_DIM), lambda h, bi, i, s: (bi * qb + i, h)),
    )
    return pl.pallas_call(
        functools.partial(_window_kernel, n=n),
        grid_spec=grid_spec,
        out_shape=jax.ShapeDtypeStruct((n_tok, W_B), F32),
        compiler_params=_params(("parallel", "parallel", "parallel"), 32),
        name="attention_window",
    )(sink, proj, proj, proj, proj, proj, proj, proj, bias)


def _outproj_kernel(oa_ref, ob_ref, ga_ref, gb_ref, w_ref, x_ref, o_ref, mix_ref):
    j = pl.program_id(1)

    @pl.when(j == 0)
    def _():
        for src, gain, lo in ((oa_ref, ga_ref, 0), (ob_ref, gb_ref, W_A)):
            o = src[...]
            ms = jnp.mean(o * o, axis=-1, keepdims=True)
            mix_ref[:, lo:lo + o.shape[1]] = (o * lax.rsqrt(ms + NORM_EPS) * gain[...]).astype(BF16)

    o_ref[...] = x_ref[...] + jnp.dot(mix_ref[...], w_ref[...], preferred_element_type=F32)


def _out_projection(oa, ob, gain_a, gain_b, w_out_bf, x2, tm, tn):
    n_tok, d = x2.shape
    return pl.pallas_call(
        _outproj_kernel,
        grid=(n_tok // tm, d // tn),
        in_specs=[
            pl.BlockSpec((tm, W_A), lambda i, j: (i, 0)),
            pl.BlockSpec((tm, W_B), lambda i, j: (i, 0)),
            pl.BlockSpec((1, W_A), lambda i, j: (0, 0)),
            pl.BlockSpec((1, W_B), lambda i, j: (0, 0)),
            pl.BlockSpec((W_A + W_B, tn), lambda i, j: (0, j)),
            pl.BlockSpec((tm, tn), lambda i, j: (i, j)),
        ],
        out_specs=pl.BlockSpec((tm, tn), lambda i, j: (i, j)),
        out_shape=jax.ShapeDtypeStruct((n_tok, d), F32),
        scratch_shapes=[pltpu.VMEM((tm, W_A + W_B), BF16)],
        compiler_params=_params(("parallel", "arbitrary"), 48),
        name="out_projection",
    )(oa, ob, gain_a.reshape(1, W_A), gain_b.reshape(1, W_B), w_out_bf, x2)


def _router_kernel(x_ref, g_ref, wr_ref, hp_ref, aff_ref):
    x = x_ref[...]
    ms = jnp.mean(x * x, axis=-1, keepdims=True)
    hb = (x * lax.rsqrt(ms + NORM_EPS) * g_ref[...]).astype(BF16)
    logits = jnp.dot(hb, wr_ref[...], preferred_element_type=F32)
    e = jnp.exp(logits - jnp.max(logits, axis=-1, keepdims=True))
    aff_ref[...] = e / jnp.sum(e, axis=-1, keepdims=True)
    half = hb.shape[1] // 2
    bits = pltpu.bitcast(hb.astype(F32), U32)
    hp_ref[...] = (bits[:, :half] >> 16) | (bits[:, half:] & jnp.uint32(0xFFFF0000))


def _router(x1, norm2, w_router_bf, tm):
    n_tok, d = x1.shape
    return pl.pallas_call(
        _router_kernel,
        grid=(n_tok // tm,),
        in_specs=[
            pl.BlockSpec((tm, d), lambda i: (i, 0)),
            pl.BlockSpec((1, d), lambda i: (0, 0)),
            pl.BlockSpec((d, N_EXPERTS), lambda i: (0, 0)),
        ],
        out_specs=[pl.BlockSpec((tm, d // 2), lambda i: (i, 0)),
                   pl.BlockSpec((tm, N_EXPERTS), lambda i: (i, 0))],
        out_shape=[jax.ShapeDtypeStruct((n_tok, d // 2), U32),
                   jax.ShapeDtypeStruct((n_tok, N_EXPERTS), F32)],
        compiler_params=_params(("parallel",), 48),
        name="router",
    )(x1, norm2.reshape(1, d), w_router_bf)


SEL_T = 128
SEL_CHUNK = 1024
RANK_BITS = 4
assert N_EXPERTS <= 1 << RANK_BITS


def _slots_kernel(aff_ref, slot_ref, tokrank_ref, tokcnt_ref, off_ref, thr_ref, need_ref, carry_ref, *, cap, n_tok):
    t = pl.program_id(0)
    n_chunks = n_tok // SEL_CHUNK

    def count(pred_fn):
        def body(c, tot):
            bits = pltpu.bitcast(aff_ref[pl.ds(c * SEL_CHUNK, SEL_CHUNK), :], I32)
            return tot + jnp.sum(pred_fn(bits).astype(I32), axis=0, keepdims=True)
        return lax.fori_loop(0, n_chunks, body, jnp.zeros((1, N_EXPERTS), I32))

    @pl.when(t == 0)
    def _():
        def bit_body(b, thr):
            cand = thr | jnp.left_shift(jnp.int32(1), 30 - b)
            return jnp.where(count(lambda bits: bits >= cand) >= cap, cand, thr)
        thr = lax.fori_loop(0, 31, bit_body, jnp.zeros((1, N_EXPERTS), I32))
        thr_ref[...] = thr
        need_ref[...] = (cap - count(lambda bits: bits > thr)).astype(F32)
        carry_ref[...] = jnp.zeros(carry_ref.shape, F32)

    a = aff_ref[pl.ds(t * SEL_T, SEL_T), :]
    bits = pltpu.bitcast(a, I32)
    thr = thr_ref[...]
    gt = bits > thr
    eq = bits == thr
    r = lax.broadcasted_iota(I32, (SEL_T, SEL_T), 0)
    c = lax.broadcasted_iota(I32, (SEL_T, SEL_T), 1)
    tri = jnp.where(r >= c, 1.0, 0.0).astype(BF16)
    eq_f = jnp.where(eq, 1.0, 0.0)
    cum_eq = jnp.dot(tri, eq_f.astype(BF16), preferred_element_type=F32) + carry_ref[0:1, :]
    sel = jnp.logical_or(gt, jnp.logical_and(eq, cum_eq <= need_ref[...]))
    sel_f = jnp.where(sel, 1.0, 0.0)
    pos = jnp.dot(tri, sel_f.astype(BF16), preferred_element_type=F32) + carry_ref[1:2, :]
    off_ref[0] = carry_ref[1:2, :].astype(I32)
    carry_ref[0:1, :] = carry_ref[0:1, :] + jnp.sum(eq_f, axis=0, keepdims=True)
    carry_ref[1:2, :] = carry_ref[1:2, :] + jnp.sum(sel_f, axis=0, keepdims=True)
    slot_ref[...] = jnp.where(sel, pos - 1.0, -1.0).astype(I32)
    er = lax.broadcasted_iota(I32, (N_EXPERTS, N_EXPERTS), 0)
    ec = lax.broadcasted_iota(I32, (N_EXPERTS, N_EXPERTS), 1)
    before = jnp.where(er < ec, 1.0, 0.0).astype(BF16)
    rank = jnp.dot(sel_f.astype(BF16), before, preferred_element_type=F32).astype(I32)
    tok = t * SEL_T + lax.broadcasted_iota(I32, (SEL_T, N_EXPERTS), 0)
    tokrank_ref[...] = tok * (1 << RANK_BITS) + rank
    tokcnt_ref[...] = jnp.broadcast_to(jnp.sum(sel_f, axis=1, keepdims=True).astype(I32), (SEL_T, N_EXPERTS))


def _lists_kernel(off_ref, slot_ref, tokrank_ref, aff_ref, idx_ref, gate_ref):
    t = pl.program_id(0)

    @pl.when(t == 0)
    def _():
        idx_ref[...] = jnp.zeros(idx_ref.shape, I32)
        gate_ref[...] = jnp.zeros(gate_ref.shape, F32)

    slot = slot_ref[...]
    tokrank = tokrank_ref[...]
    a = aff_ref[...]
    for e in range(N_EXPERTS):
        base = pl.multiple_of((off_ref[t * N_EXPERTS + e] // LANE) * LANE, LANE)
        lane = base + lax.broadcasted_iota(I32, (1, 2 * LANE), 1)
        match = slot[:, e:e + 1] == lane
        win = (slice(e, e + 1), pl.ds(base, 2 * LANE))
        idx_ref[win] += jnp.sum(jnp.where(match, tokrank[:, e:e + 1], 0), axis=0, keepdims=True)
        gate_ref[win] += jnp.sum(jnp.where(match, a[:, e:e + 1], 0.0), axis=0, keepdims=True)


def _select(aff, cap):
    n_tok = aff.shape[0]
    nt = n_tok // SEL_T
    tile = pl.BlockSpec((SEL_T, N_EXPERTS), lambda t: (t, 0))
    tile_shape = jax.ShapeDtypeStruct((n_tok, N_EXPERTS), I32)
    slot, tokrank, tokcnt, offs = pl.pallas_call(
        functools.partial(_slots_kernel, cap=cap, n_tok=n_tok),
        grid=(nt,),
        in_specs=[pl.BlockSpec((n_tok, N_EXPERTS), lambda t: (0, 0))],
        out_specs=[tile, tile, tile, pl.BlockSpec((1, 1, N_EXPERTS), lambda t: (t, 0, 0))],
        out_shape=[tile_shape, tile_shape, tile_shape, jax.ShapeDtypeStruct((nt, 1, N_EXPERTS), I32)],
        scratch_shapes=[pltpu.VMEM((1, N_EXPERTS), I32), pltpu.VMEM((1, N_EXPERTS), F32),
                        pltpu.VMEM((2, N_EXPERTS), F32)],
        compiler_params=_params(("arbitrary",), 32),
        name="expert_slots",
    )(aff)
    offs = offs.reshape(nt, N_EXPERTS)
    width = cap + 2 * LANE
    tile_p = pl.BlockSpec((SEL_T, N_EXPERTS), lambda t, o: (t, 0))
    idx, gates = pl.pallas_call(
        _lists_kernel,
        grid_spec=pltpu.PrefetchScalarGridSpec(
            num_scalar_prefetch=1, grid=(nt,),
            in_specs=[tile_p, tile_p, tile_p],
            out_specs=[pl.BlockSpec((N_EXPERTS, width), lambda t, o: (0, 0)),
                       pl.BlockSpec((N_EXPERTS, width), lambda t, o: (0, 0))]),
        out_shape=[jax.ShapeDtypeStruct((N_EXPERTS, width), I32),
                   jax.ShapeDtypeStruct((N_EXPERTS, width), F32)],
        compiler_params=_params(("arbitrary",), 32),
        name="expert_lists",
    )(offs.reshape(-1), slot, tokrank, aff)
    return idx[:, :cap], gates[:, :cap], offs, tokcnt


def _gather_kernel(idx_ref, hp_ref, xe_ref, sem, *, rows):
    def row_copy(s):
        tok = idx_ref[0, 0, s] >> RANK_BITS
        return pltpu.make_async_copy(hp_ref.at[pl.ds(tok, 1)], xe_ref.at[0, pl.ds(s, 1)], sem)

    def start(s, carry):
        row_copy(s).start()
        return carry

    def wait(s, carry):
        row_copy(s).wait()
        return carry

    lax.fori_loop(0, rows, start, 0)
    lax.fori_loop(0, rows, wait, 0)


def _gather(idx, hp, cap, rows):
    half = hp.shape[1]
    n_blocks = cap // rows
    idx3 = idx.reshape(N_EXPERTS * n_blocks, 1, rows)
    return pl.pallas_call(
        functools.partial(_gather_kernel, rows=rows),
        grid=(N_EXPERTS, n_blocks),
        in_specs=[pl.BlockSpec((1, 1, rows), lambda e, r: (e * n_blocks + r, 0, 0), memory_space=pltpu.SMEM),
                  pl.BlockSpec(memory_space=pl.ANY)],
        out_specs=pl.BlockSpec((1, rows, half), lambda e, r: (e, r, 0)),
        out_shape=jax.ShapeDtypeStruct((N_EXPERTS, cap, half), U32),
        scratch_shapes=[pltpu.SemaphoreType.DMA(())],
        compiler_params=_params(("arbitrary", "arbitrary"), 32),
        name="expert_gather",
    )(idx3, hp)


def _unpack(xp):
    lo = pltpu.bitcast(xp << 16, F32).astype(BF16)
    hi = pltpu.bitcast(xp & jnp.uint32(0xFFFF0000), F32).astype(BF16)
    return lo, hi


def _ffn_up_kernel(xe_ref, wg_ref, wu_ref, h_ref):
    lo, hi = _unpack(xe_ref[0])
    half = lo.shape[1]

    def proj(w_ref):
        return (jnp.dot(lo, w_ref[0, :half, :], preferred_element_type=F32)
                + jnp.dot(hi, w_ref[0, half:, :], preferred_element_type=F32))

    gate = proj(wg_ref)
    up = proj(wu_ref)
    h_ref[0] = (gate * jax.nn.sigmoid(gate) * up).astype(BF16)


def _ffn_up(xe, wg_bf, wu_bf, tr, tf):
    _, cap, half = xe.shape
    d, f = wg_bf.shape[1], wg_bf.shape[2]
    return pl.pallas_call(
        _ffn_up_kernel,
        grid=(N_EXPERTS, f // tf, cap // tr),
        in_specs=[pl.BlockSpec((1, tr, half), lambda e, j, r: (e, r, 0)),
                  pl.BlockSpec((1, d, tf), lambda e, j, r: (e, 0, j)),
                  pl.BlockSpec((1, d, tf), lambda e, j, r: (e, 0, j))],
        out_specs=pl.BlockSpec((1, tr, tf), lambda e, j, r: (e, r, j)),
        out_shape=jax.ShapeDtypeStruct((N_EXPERTS, cap, f), BF16),
        compiler_params=_params(("parallel", "parallel", "arbitrary"), 56),
        name="expert_ffn_up",
    )(xe, wg_bf, wu_bf)


def _pack_halves(x):
    w = x.shape[1] // 2
    bits = pltpu.bitcast(x.astype(BF16).astype(F32), U32)
    return (bits[:, :w] >> 16) | (bits[:, w:] & jnp.uint32(0xFFFF0000))


def _ffn_down_kernel(h_ref, wd_ref, g_ref, o_ref):
    out = jnp.dot(h_ref[0], wd_ref[0], preferred_element_type=F32)
    o_ref[0] = _pack_halves(g_ref[0] * out)


def _ffn_down(hid, wd_bf, gates, tr, tn):
    _, cap, f = hid.shape
    d = wd_bf.shape[2]
    return pl.pallas_call(
        _ffn_down_kernel,
        grid=(N_EXPERTS, d // tn, cap // tr),
        in_specs=[pl.BlockSpec((1, tr, f), lambda e, j, r: (e, r, 0)),
                  pl.BlockSpec((1, f, tn), lambda e, j, r: (e, 0, j)),
                  pl.BlockSpec((1, tr, 1), lambda e, j, r: (e, r, 0))],
        out_specs=pl.BlockSpec((1, tr, tn // 2), lambda e, j, r: (e, r, j)),
        out_shape=jax.ShapeDtypeStruct((N_EXPERTS, cap, d // 2), U32),
        compiler_params=_params(("parallel", "parallel", "arbitrary"), 48),
        name="expert_ffn_down",
    )(hid, wd_bf, gates.reshape(N_EXPERTS, cap, 1))


def _combine_kernel(lo_ref, hi_ref, maxc_ref, idx_ref, x_ref, cnt_ref, con_ref, y_ref, stage_ref, sem,
                    *, nt, cap, tn):
    t = pl.program_id(0)

    @pl.when(t == 0)
    def _():
        stage_ref[...] = jnp.zeros(stage_ref.shape, U32)

    def row_copy(row):
        v = idx_ref[row]
        tok = (v >> RANK_BITS) - t * SEL_T
        return pltpu.make_async_copy(con_ref.at[pl.ds(row, 1)],
                                     stage_ref.at[v & ((1 << RANK_BITS) - 1), pl.ds(tok, 1)], sem)

    def for_rows(fn):
        def per_row(row, carry):
            fn(row_copy(row))
            return carry

        def per_expert(e, carry):
            k = e * nt + t
            return lax.fori_loop(e * cap + lo_ref[k], e * cap + hi_ref[k], per_row, carry)

        lax.fori_loop(0, N_EXPERTS, per_expert, 0)

    for_rows(lambda copy: copy.start())
    y_ref[...] = x_ref[...]
    for_rows(lambda copy: copy.wait())

    half = tn // 2
    cnt = jnp.broadcast_to(cnt_ref[:, 0:1], (SEL_T, half))

    def add_rank(k, carry):
        live = k < cnt
        for j in range(y_ref.shape[1] // tn):
            xp = stage_ref[k, :, j * half:(j + 1) * half]
            lo = pltpu.bitcast(xp << 16, F32)
            hi = pltpu.bitcast(xp & jnp.uint32(0xFFFF0000), F32)
            y_ref[:, j * tn:j * tn + half] += jnp.where(live, lo, 0.0)
            y_ref[:, j * tn + half:(j + 1) * tn] += jnp.where(live, hi, 0.0)
        return carry

    lax.fori_loop(0, maxc_ref[t], add_rank, 0)


def _combine(x1, idx, contrib, offs, tokcnt, cap, tn):
    n_tok, d = x1.shape
    nt = n_tok // SEL_T
    lo = jnp.transpose(offs)
    hi = jnp.concatenate([lo[:, 1:], jnp.full((N_EXPERTS, 1), cap, I32)], axis=1)
    maxc = jnp.max(tokcnt[:, 0].reshape(nt, SEL_T), axis=1)
    grid_spec = pltpu.PrefetchScalarGridSpec(
        num_scalar_prefetch=4,
        grid=(nt,),
        in_specs=[
            pl.BlockSpec((SEL_T, d), lambda t, *_: (t, 0)),
            pl.BlockSpec((SEL_T, N_EXPERTS), lambda t, *_: (t, 0)),
            pl.BlockSpec(memory_space=pl.ANY),
        ],
        out_specs=pl.BlockSpec((SEL_T, d), lambda t, *_: (t, 0)),
        scratch_shapes=[pltpu.VMEM((N_EXPERTS, SEL_T, d // 2), U32), pltpu.SemaphoreType.DMA(())],
    )
    return pl.pallas_call(
        functools.partial(_combine_kernel, nt=nt, cap=cap, tn=tn),
        grid_spec=grid_spec,
        out_shape=jax.ShapeDtypeStruct((n_tok, d), F32),
        compiler_params=_params(("arbitrary",), 40),
        name="expert_combine",
    )(lo.reshape(-1), hi.reshape(-1), maxc, idx.reshape(-1), x1, tokcnt,
      contrib.reshape(N_EXPERTS * cap, d // 2))


DOWN_TN = 1024


def _tiles(n_tok, n_seq, cap):
    return dict(
        tm=min(512, n_seq),
        rows=min(512, cap),
        tr=min(1024, cap),
    )


def _encoder_layer(x, p):
    b, n, d = x.shape
    n_tok = b * n
    cap = EC_FACTOR * n_tok // N_EXPERTS
    tl = _tiles(n_tok, n, cap)
    x2 = x.reshape(n_tok, d)

    proj = _in_projection(x2, p["norm1"], p["w_in"], p["gains"], p["cos"][n], p["sin"][n], n, tl["tm"])
    oa = _attention_a(proj, b, n, ATT_TQ)
    ob = _attention_b(proj, p["sink"], p["bias"], b, n)
    x1 = _out_projection(oa, ob, p["out_norm_a"], p["out_norm_b"], p["w_out"], x2, tl["tm"], 512)

    hp, aff = _router(x1, p["norm2"], p["w_router"], tl["tm"])
    idx, gates, offs, tokcnt = _select(aff, cap)
    xe = _gather(idx, hp, cap, tl["rows"])
    hid = _ffn_up(xe, p["w_gate"], p["w_up"], tl["tr"], 512)
    contrib = _ffn_down(hid, p["w_down"], gates, tl["tr"], DOWN_TN)
    y = _combine(x1, idx, contrib, offs, tokcnt, cap, DOWN_TN)
    return y.reshape(b, n, d)


def kernel(x_prompt, x_sample, norm1, w_in, q_norm_a, k_norm_a, q_norm_b, k_norm_b, sink_b, rel_bias,
           out_norm_a, out_norm_b, w_out, norm2, w_router, w_gate, w_up, w_down):
    depth = norm1.shape[0]
    seqs = sorted({x_prompt.shape[1], x_sample.shape[1]})
    tables = {n: _rope_tables(n) for n in seqs}
    bias = _window_bias(rel_bias)
    ones = jnp.ones((KVW,), F32)

    layers = []
    for l in range(depth):
        gains = jnp.concatenate([jnp.tile(q_norm_a[l], H_A), jnp.tile(k_norm_a[l], KV_A), ones,
                                 jnp.tile(q_norm_b[l], H_B), jnp.tile(k_norm_b[l], KV_B), ones])
        layers.append(dict(
            norm1=norm1[l], w_in=w_in[l].astype(BF16), gains=gains.reshape(1, IN_WIDTH),
            cos={n: tables[n][0] for n in seqs}, sin={n: tables[n][1] for n in seqs},
            sink=sink_b[l], bias=bias,
            out_norm_a=out_norm_a[l], out_norm_b=out_norm_b[l], w_out=w_out[l].astype(BF16),
            norm2=norm2[l], w_router=w_router[l].astype(BF16),
            w_gate=w_gate[l].astype(BF16), w_up=w_up[l].astype(BF16), w_down=w_down[l].astype(BF16),
        ))

    def run(x):
        for p in layers:
            x = _encoder_layer(x, p)
        return x

    return (run(x_prompt), run(x_sample))
```

```python
import functools
import math

import jax
import jax.numpy as jnp
from jax import lax
from jax.experimental import pallas as pl
from jax.experimental.pallas import tpu as pltpu

F32 = jnp.float32
BF16 = jnp.bfloat16
I32 = jnp.int32
U32 = jnp.uint32

HEAD_DIM = 128
H_A = 16
KV_A = 4
H_B = 16
KV_B = 4
GROUP = 4
W_A = H_A * HEAD_DIM
W_B = H_B * HEAD_DIM
KVW = KV_A * HEAD_DIM
IN_WIDTH = W_A + 2 * KVW + W_B + 2 * KVW
GRID_W = 64
HALF_ROT = HEAD_DIM // 2
ROPE_THETA = 10000.0
REL_BUCKETS = 32
REL_MAX_DIST = 128
WINDOW = 128
N_EXPERTS = 16
EC_FACTOR = 2
NORM_EPS = 1e-6
NEG_INF = -1e30
SCALE = 1.0 / math.sqrt(HEAD_DIM)

V7X_VMEM_BYTES = 64 * 1024 * 1024
LANE = 128


def _vmem(mib):
    assert mib * 1024 * 1024 < V7X_VMEM_BYTES
    return mib * 1024 * 1024


def _params(sem, mib):
    return pltpu.CompilerParams(dimension_semantics=sem, vmem_limit_bytes=_vmem(mib))


IN_TN = 512
_ROPE_TILES = 5
_V_TILES = (5, 11)


def _inproj_kernel(x_ref, g1_ref, w_ref, gain_ref, cos_ref, sin_ref, o_ref, h_ref):
    j = pl.program_id(1)

    @pl.when(j == 0)
    def _():
        x = x_ref[...]
        ms = jnp.mean(x * x, axis=-1, keepdims=True)
        h_ref[...] = (x * lax.rsqrt(ms + NORM_EPS) * g1_ref[...]).astype(BF16)

    acc = jnp.dot(h_ref[...], w_ref[...], preferred_element_type=F32)
    is_v = jnp.logical_or(j == _V_TILES[0], j == _V_TILES[1])
    is_rope = j < _ROPE_TILES

    def head_norm():
        outs = []
        for hh in range(IN_TN // HEAD_DIM):
            a = acc[:, hh * HEAD_DIM:(hh + 1) * HEAD_DIM]
            ms = jnp.mean(a * a, axis=-1, keepdims=True)
            outs.append(a * lax.rsqrt(ms + NORM_EPS) * gain_ref[:, hh * HEAD_DIM:(hh + 1) * HEAD_DIM])
        return jnp.concatenate(outs, axis=-1)

    @pl.when(is_v)
    def _():
        o_ref[...] = acc.astype(BF16)

    @pl.when(jnp.logical_and(jnp.logical_not(is_v), jnp.logical_not(is_rope)))
    def _():
        is_qb = j < (IN_WIDTH - 2 * KVW) // IN_TN
        o_ref[...] = (head_norm() * jnp.where(is_qb, SCALE * LOG2E, 1.0)).astype(BF16)

    @pl.when(is_rope)
    def _():
        y = head_norm()
        reps = IN_TN // HEAD_DIM
        c = jnp.concatenate([cos_ref[...]] * reps, axis=-1)
        s = jnp.concatenate([sin_ref[...]] * reps, axis=-1)
        lane = lax.broadcasted_iota(I32, y.shape, 1)
        first_half = (lane % (2 * (HALF_ROT // 2))) < (HALF_ROT // 2)
        partner = jnp.where(first_half,
                            pltpu.roll(y, IN_TN - HALF_ROT // 2, axis=1),
                            pltpu.roll(y, HALF_ROT // 2, axis=1))
        qscale = jnp.where(j < W_A // IN_TN, SCALE * LOG2E, 1.0)
        o_ref[...] = ((y * c + partner * s) * qscale).astype(BF16)


def _rope_tables(n):
    rows = n // GRID_W
    row = jnp.repeat(jnp.arange(rows, dtype=F32), GRID_W)
    col = jnp.tile(jnp.arange(GRID_W, dtype=F32), rows)
    inv = ROPE_THETA ** (-jnp.arange(0, HALF_ROT, 2, dtype=F32) / HALF_ROT)
    ar, ac = row[:, None] * inv, col[:, None] * inv
    cr, sr, cc, sc = jnp.cos(ar), jnp.sin(ar), jnp.cos(ac), jnp.sin(ac)
    cos_t = jnp.concatenate([cr, cr, cc, cc], axis=-1)
    sin_t = jnp.concatenate([-sr, sr, -sc, sc], axis=-1)
    return cos_t, sin_t


def _in_projection(x2, norm1, w_in_bf, gains, cos_t, sin_t, n_seq, tm):
    n_tok, d = x2.shape
    seq_blocks = n_seq // tm
    return pl.pallas_call(
        _inproj_kernel,
        grid=(n_tok // tm, IN_WIDTH // IN_TN),
        in_specs=[
            pl.BlockSpec((tm, d), lambda i, j: (i, 0)),
            pl.BlockSpec((1, d), lambda i, j: (0, 0)),
            pl.BlockSpec((d, IN_TN), lambda i, j: (0, j)),
            pl.BlockSpec((1, IN_TN), lambda i, j: (0, j)),
            pl.BlockSpec((tm, HEAD_DIM), lambda i, j: (i % seq_blocks, 0)),
            pl.BlockSpec((tm, HEAD_DIM), lambda i, j: (i % seq_blocks, 0)),
        ],
        out_specs=pl.BlockSpec((tm, IN_TN), lambda i, j: (i, j)),
        out_shape=jax.ShapeDtypeStruct((n_tok, IN_WIDTH), BF16),
        scratch_shapes=[pltpu.VMEM((tm, d), BF16)],
        compiler_params=_params(("parallel", "arbitrary"), 48),
        name="in_projection",
    )(x2, norm1.reshape(1, d), w_in_bf, gains, cos_t, sin_t)


ATT_TQ = 128
ATT_TKC = 256
ATT_UNROLL = 16
LOG2E = 1.4426950408889634


def _flash_kernel(q_ref, k_ref, v_ref, o_ref, *, n_chunks, unroll):
    q = q_ref[...]
    tq = q.shape[0]
    q_all = jnp.concatenate([q[:, g * HEAD_DIM:(g + 1) * HEAD_DIM] for g in range(GROUP)], axis=0)
    nq = GROUP * tq

    def body(c, carry):
        m, l, acc = carry
        starts = [pl.multiple_of((c * unroll + u) * ATT_TKC, ATT_TKC) for u in range(unroll)]
        scores = [lax.dot_general(k_ref[pl.ds(st, ATT_TKC), :], q_all, (((1,), (1,)), ((), ())),
                                  preferred_element_type=F32) for st in starts]
        for st, s in zip(starts, scores):
            m_new = jnp.maximum(m, jnp.max(s, axis=0, keepdims=True))
            alpha = jnp.exp2(m - m_new)
            p = jnp.exp2(s - m_new)
            l = alpha * l + jnp.sum(p, axis=0, keepdims=True)
            pv = lax.dot_general(v_ref[pl.ds(st, ATT_TKC), :], p.astype(BF16), (((0,), (0,)), ((), ())),
                                 preferred_element_type=F32)
            acc = alpha * acc + pv
            m = m_new
        return m, l, acc

    init = (jnp.full((1, nq), -jnp.inf, F32), jnp.zeros((1, nq), F32), jnp.zeros((HEAD_DIM, nq), F32))
    _, l, acc = lax.fori_loop(0, n_chunks // unroll, body, init)
    o_t = acc / l
    for g in range(GROUP):
        o_ref[:, g * HEAD_DIM:(g + 1) * HEAD_DIM] = o_t[:, g * tq:(g + 1) * tq].T


def _attention_a(proj, b, n, tq):
    n_tok = b * n
    qb = n // tq
    k_col = W_A // HEAD_DIM
    v_col = (W_A + KVW) // HEAD_DIM
    n_chunks = n // ATT_TKC
    unroll = math.gcd(n_chunks, ATT_UNROLL)
    return pl.pallas_call(
        functools.partial(_flash_kernel, n_chunks=n_chunks, unroll=unroll),
        grid=(b, KV_A, qb),
        in_specs=[
            pl.BlockSpec((tq, GROUP * HEAD_DIM), lambda bi, h, qi: (bi * qb + qi, h)),
            pl.BlockSpec((n, HEAD_DIM), lambda bi, h, qi: (bi, k_col + h)),
            pl.BlockSpec((n, HEAD_DIM), lambda bi, h, qi: (bi, v_col + h)),
        ],
        out_specs=pl.BlockSpec((tq, GROUP * HEAD_DIM), lambda bi, h, qi: (bi * qb + qi, h)),
        out_shape=jax.ShapeDtypeStruct((n_tok, W_A), F32),
        compiler_params=_params(("parallel", "parallel", "arbitrary"), 48),
        name="attention_global",
    )(proj, proj, proj)


WIN_TQ = 2 * WINDOW
WIN_TK = WIN_TQ + 2 * WINDOW


def _t5_buckets(rel):
    nb = REL_BUCKETS // 2
    max_exact = nb // 2
    ret = jnp.where(rel > 0, nb, 0)
    n = jnp.abs(rel)
    nf = jnp.maximum(n, 1).astype(F32)
    large = max_exact + (jnp.log(nf / max_exact) / math.log(REL_MAX_DIST / max_exact)
                         * (nb - max_exact)).astype(I32)
    large = jnp.minimum(large, nb - 1)
    return ret + jnp.where(n < max_exact, n, large)


def _window_bias(rel_bias):
    span = WIN_TQ + WIN_TK
    kk = jnp.arange(span)
    rel = jnp.where(kk < WIN_TK, kk, kk - span) - WINDOW
    tbl = jnp.where((jnp.abs(rel) <= WINDOW)[:, None], rel_bias[_t5_buckets(rel)].astype(F32), NEG_INF)
    flat = jnp.tile(tbl.T, (1, WIN_TQ))[:, :WIN_TQ * (span - 1)]
    bias = flat.reshape(H_B, WIN_TQ, span - 1)[:, :, :WIN_TK]
    j = jnp.arange(WIN_TK)
    before, after = j < WINDOW, j >= WINDOW + WIN_TQ
    masks = jnp.stack([jnp.zeros_like(before), before, after, jnp.logical_or(before, after)])
    bias = jnp.where(masks[:, None, None, :], NEG_INF, bias[None])
    bias = bias.reshape(4, KV_B, GROUP, WIN_TQ, WIN_TK).transpose(0, 1, 4, 2, 3)
    return bias.reshape(4, KV_B, WIN_TK, GROUP * WIN_TQ) * LOG2E


def _window_kernel(sink_ref, q_ref, kp_ref, kc_ref, kn_ref, vp_ref, vc_ref, vn_ref, bias_ref, o_ref):
    h = pl.program_id(0)
    q = q_ref[...]
    q_all = jnp.concatenate([q[:, g * HEAD_DIM:(g + 1) * HEAD_DIM] for g in range(GROUP)], axis=0)
    kcat = jnp.concatenate([kp_ref[...], kc_ref[...], kn_ref[...]], axis=0)
    vcat = jnp.concatenate([vp_ref[...], vc_ref[...], vn_ref[...]], axis=0)
    s = lax.dot_general(kcat, q_all, (((1,), (1,)), ((), ())), preferred_element_type=F32) + bias_ref[0, 0]
    head = lax.broadcasted_iota(I32, (1, GROUP * WIN_TQ), 1) // WIN_TQ
    snk = jnp.zeros((1, GROUP * WIN_TQ), F32)
    for g in range(GROUP):
        snk = jnp.where(head == g, sink_ref[h * GROUP + g] * LOG2E, snk)
    m = jnp.maximum(jnp.max(s, axis=0, keepdims=True), snk)
    e = jnp.exp2(s - m)
    denom = jnp.sum(e, axis=0, keepdims=True) + jnp.exp2(snk - m)
    o_t = lax.dot_general(vcat, e.astype(BF16), (((0,), (0,)), ((), ())), preferred_element_type=F32) / denom
    for g in range(GROUP):
        o_ref[:, g * HEAD_DIM:(g + 1) * HEAD_DIM] = o_t[:, g * WIN_TQ:(g + 1) * WIN_TQ].T


def _attention_b(proj, sink, bias, b, n):
    n_tok = b * n
    qb = n // WIN_TQ
    nb128 = n // WINDOW
    base = W_A + 2 * KVW
    q_col = base // (GROUP * HEAD_DIM)
    k_col = (base + W_B) // HEAD_DIM
    v_col = (base + W_B + KVW) // HEAD_DIM

    def prev_map(col):
        return lambda h, bi, i, s: (bi * nb128 + jnp.maximum(2 * i - 1, 0), col + h)

    def cur_map(col):
        return lambda h, bi, i, s: (bi * qb + i, col + h)

    def next_map(col):
        return lambda h, bi, i, s: (bi * nb128 + jnp.minimum(2 * i + 2, nb128 - 1), col + h)

    grid_spec = pltpu.PrefetchScalarGridSpec(
        num_scalar_prefetch=1,
        grid=(KV_B, b, qb),
        in_specs=[
            pl.BlockSpec((WIN_TQ, GROUP * HEAD_DIM), lambda h, bi, i, s: (bi * qb + i, q_col + h)),
            pl.BlockSpec((WINDOW, HEAD_DIM), prev_map(k_col)),
            pl.BlockSpec((WIN_TQ, HEAD_DIM), cur_map(k_col)),
            pl.BlockSpec((WINDOW, HEAD_DIM), next_map(k_col)),
            pl.BlockSpec((WINDOW, HEAD_DIM), prev_map(v_col)),
            pl.BlockSpec((WIN_TQ, HEAD_DIM), cur_map(v_col)),
            pl.BlockSpec((WINDOW, HEAD_DIM), next_map(v_col)),
            pl.BlockSpec((1, 1, WIN_TK, GROUP * WIN_TQ),
                         lambda h, bi, i, s: (jnp.where(i == 0, 1, 0) + jnp.where(i == qb - 1, 2, 0), h, 0, 0)),
        ],
        out_specs=pl.BlockSpec((WIN_TQ, GROUP * HEAD_DIM), lambda h, bi, i, s: (bi * qb + i, h)),
    )
    return pl.pallas_call(
        _window_kernel,
        grid_spec=grid_spec,
        out_shape=jax.ShapeDtypeStruct((n_tok, W_B), F32),
        compiler_params=_params(("parallel", "parallel", "parallel"), 32),
        name="attention_window",
    )(sink, proj, proj, proj, proj, proj, proj, proj, bias)


def _outproj_kernel(oa_ref, ob_ref, ga_ref, gb_ref, w_ref, x_ref, o_ref, mix_ref):
    j = pl.program_id(1)

    @pl.when(j == 0)
    def _():
        for src, gain, lo in ((oa_ref, ga_ref, 0), (ob_ref, gb_ref, W_A)):
            o = src[...]
            ms = jnp.mean(o * o, axis=-1, keepdims=True)
            mix_ref[:, lo:lo + o.shape[1]] = (o * lax.rsqrt(ms + NORM_EPS) * gain[...]).astype(BF16)

    o_ref[...] = x_ref[...] + jnp.dot(mix_ref[...], w_ref[...], preferred_element_type=F32)


def _out_projection(oa, ob, gain_a, gain_b, w_out_bf, x2, tm, tn):
    n_tok, d = x2.shape
    return pl.pallas_call(
        _outproj_kernel,
        grid=(n_tok // tm, d // tn),
        in_specs=[
            pl.BlockSpec((tm, W_A), lambda i, j: (i, 0)),
            pl.BlockSpec((tm, W_B), lambda i, j: (i, 0)),
            pl.BlockSpec((1, W_A), lambda i, j: (0, 0)),
            pl.BlockSpec((1, W_B), lambda i, j: (0, 0)),
            pl.BlockSpec((W_A + W_B, tn), lambda i, j: (0, j)),
            pl.BlockSpec((tm, tn), lambda i, j: (i, j)),
        ],
        out_specs=pl.BlockSpec((tm, tn), lambda i, j: (i, j)),
        out_shape=jax.ShapeDtypeStruct((n_tok, d), F32),
        scratch_shapes=[pltpu.VMEM((tm, W_A + W_B), BF16)],
        compiler_params=_params(("parallel", "arbitrary"), 48),
        name="out_projection",
    )(oa, ob, gain_a.reshape(1, W_A), gain_b.reshape(1, W_B), w_out_bf, x2)


def _router_kernel(x_ref, g_ref, wr_ref, hp_ref, aff_ref):
    x = x_ref[...]
    ms = jnp.mean(x * x, axis=-1, keepdims=True)
    hb = (x * lax.rsqrt(ms + NORM_EPS) * g_ref[...]).astype(BF16)
    logits = jnp.dot(hb, wr_ref[...], preferred_element_type=F32)
    e = jnp.exp(logits - jnp.max(logits, axis=-1, keepdims=True))
    aff_ref[...] = e / jnp.sum(e, axis=-1, keepdims=True)
    half = hb.shape[1] // 2
    bits = pltpu.bitcast(hb.astype(F32), U32)
    hp_ref[...] = (bits[:, :half] >> 16) | (bits[:, half:] & jnp.uint32(0xFFFF0000))


def _router(x1, norm2, w_router_bf, tm):
    n_tok, d = x1.shape
    return pl.pallas_call(
        _router_kernel,
        grid=(n_tok // tm,),
        in_specs=[
            pl.BlockSpec((tm, d), lambda i: (i, 0)),
            pl.BlockSpec((1, d), lambda i: (0, 0)),
            pl.BlockSpec((d, N_EXPERTS), lambda i: (0, 0)),
        ],
        out_specs=[pl.BlockSpec((tm, d // 2), lambda i: (i, 0)),
                   pl.BlockSpec((tm, N_EXPERTS), lambda i: (i, 0))],
        out_shape=[jax.ShapeDtypeStruct((n_tok, d // 2), U32),
                   jax.ShapeDtypeStruct((n_tok, N_EXPERTS), F32)],
        compiler_params=_params(("parallel",), 48),
        name="router",
    )(x1, norm2.reshape(1, d), w_router_bf)


SEL_T = 128
SEL_CHUNK = 1024
RANK_BITS = 4
assert N_EXPERTS <= 1 << RANK_BITS


def _slots_kernel(aff_ref, slot_ref, tokrank_ref, tokcnt_ref, off_ref, thr_ref, need_ref, carry_ref, *, cap, n_tok):
    t = pl.program_id(0)
    n_chunks = n_tok // SEL_CHUNK

    def count(pred_fn):
        def body(c, tot):
            bits = pltpu.bitcast(aff_ref[pl.ds(c * SEL_CHUNK, SEL_CHUNK), :], I32)
            return tot + jnp.sum(pred_fn(bits).astype(I32), axis=0, keepdims=True)
        return lax.fori_loop(0, n_chunks, body, jnp.zeros((1, N_EXPERTS), I32))

    @pl.when(t == 0)
    def _():
        def bit_body(b, thr):
            cand = thr | jnp.left_shift(jnp.int32(1), 30 - b)
            return jnp.where(count(lambda bits: bits >= cand) >= cap, cand, thr)
        thr = lax.fori_loop(0, 31, bit_body, jnp.zeros((1, N_EXPERTS), I32))
        thr_ref[...] = thr
        need_ref[...] = (cap - count(lambda bits: bits > thr)).astype(F32)
        carry_ref[...] = jnp.zeros(carry_ref.shape, F32)

    a = aff_ref[pl.ds(t * SEL_T, SEL_T), :]
    bits = pltpu.bitcast(a, I32)
    thr = thr_ref[...]
    gt = bits > thr
    eq = bits == thr
    r = lax.broadcasted_iota(I32, (SEL_T, SEL_T), 0)
    c = lax.broadcasted_iota(I32, (SEL_T, SEL_T), 1)
    tri = jnp.where(r >= c, 1.0, 0.0).astype(BF16)
    eq_f = jnp.where(eq, 1.0, 0.0)
    cum_eq = jnp.dot(tri, eq_f.astype(BF16), preferred_element_type=F32) + carry_ref[0:1, :]
    sel = jnp.logical_or(gt, jnp.logical_and(eq, cum_eq <= need_ref[...]))
    sel_f = jnp.where(sel, 1.0, 0.0)
    pos = jnp.dot(tri, sel_f.astype(BF16), preferred_element_type=F32) + carry_ref[1:2, :]
    off_ref[0] = carry_ref[1:2, :].astype(I32)
    carry_ref[0:1, :] = carry_ref[0:1, :] + jnp.sum(eq_f, axis=0, keepdims=True)
    carry_ref[1:2, :] = carry_ref[1:2, :] + jnp.sum(sel_f, axis=0, keepdims=True)
    slot_ref[...] = jnp.where(sel, pos - 1.0, -1.0).astype(I32)
    er = lax.broadcasted_iota(I32, (N_EXPERTS, N_EXPERTS), 0)
    ec = lax.broadcasted_iota(I32, (N_EXPERTS, N_EXPERTS), 1)
    before = jnp.where(er < ec, 1.0, 0.0).astype(BF16)
    rank = jnp.dot(sel_f.astype(BF16), before, preferred_element_type=F32).astype(I32)
    tok = t * SEL_T + lax.broadcasted_iota(I32, (SEL_T, N_EXPERTS), 0)
    tokrank_ref[...] = tok * (1 << RANK_BITS) + rank
    tokcnt_ref[...] = jnp.broadcast_to(jnp.sum(sel_f, axis=1, keepdims=True).astype(I32), (SEL_T, N_EXPERTS))


def _lists_kernel(off_ref, slot_ref, tokrank_ref, aff_ref, idx_ref, gate_ref):
    t = pl.program_id(0)

    @pl.when(t == 0)
    def _():
        idx_ref[...] = jnp.zeros(idx_ref.shape, I32)
        gate_ref[...] = jnp.zeros(gate_ref.shape, F32)

    slot = slot_ref[...]
    tokrank = tokrank_ref[...]
    a = aff_ref[...]
    for e in range(N_EXPERTS):
        base = pl.multiple_of((off_ref[t * N_EXPERTS + e] // LANE) * LANE, LANE)
        lane = base + lax.broadcasted_iota(I32, (1, 2 * LANE), 1)
        match = slot[:, e:e + 1] == lane
        win = (slice(e, e + 1), pl.ds(base, 2 * LANE))
        idx_ref[win] += jnp.sum(jnp.where(match, tokrank[:, e:e + 1], 0), axis=0, keepdims=True)
        gate_ref[win] += jnp.sum(jnp.where(match, a[:, e:e + 1], 0.0), axis=0, keepdims=True)


def _select(aff, cap):
    n_tok = aff.shape[0]
    nt = n_tok // SEL_T
    tile = pl.BlockSpec((SEL_T, N_EXPERTS), lambda t: (t, 0))
    tile_shape = jax.ShapeDtypeStruct((n_tok, N_EXPERTS), I32)
    slot, tokrank, tokcnt, offs = pl.pallas_call(
        functools.partial(_slots_kernel, cap=cap, n_tok=n_tok),
        grid=(nt,),
        in_specs=[pl.BlockSpec((n_tok, N_EXPERTS), lambda t: (0, 0))],
        out_specs=[tile, tile, tile, pl.BlockSpec((1, 1, N_EXPERTS), lambda t: (t, 0, 0))],
        out_shape=[tile_shape, tile_shape, tile_shape, jax.ShapeDtypeStruct((nt, 1, N_EXPERTS), I32)],
        scratch_shapes=[pltpu.VMEM((1, N_EXPERTS), I32), pltpu.VMEM((1, N_EXPERTS), F32),
                        pltpu.VMEM((2, N_EXPERTS), F32)],
        compiler_params=_params(("arbitrary",), 32),
        name="expert_slots",
    )(aff)
    offs = offs.reshape(nt, N_EXPERTS)
    width = cap + 2 * LANE
    tile_p = pl.BlockSpec((SEL_T, N_EXPERTS), lambda t, o: (t, 0))
    idx, gates = pl.pallas_call(
        _lists_kernel,
        grid_spec=pltpu.PrefetchScalarGridSpec(
            num_scalar_prefetch=1, grid=(nt,),
            in_specs=[tile_p, tile_p, tile_p],
            out_specs=[pl.BlockSpec((N_EXPERTS, width), lambda t, o: (0, 0)),
                       pl.BlockSpec((N_EXPERTS, width), lambda t, o: (0, 0))]),
        out_shape=[jax.ShapeDtypeStruct((N_EXPERTS, width), I32),
                   jax.ShapeDtypeStruct((N_EXPERTS, width), F32)],
        compiler_params=_params(("arbitrary",), 32),
        name="expert_lists",
    )(offs.reshape(-1), slot, tokrank, aff)
    return idx[:, :cap], gates[:, :cap], offs, tokcnt


def _gather_kernel(idx_ref, hp_ref, xe_ref, sem, *, rows):
    def row_copy(s):
        tok = idx_ref[0, 0, s] >> RANK_BITS
        return pltpu.make_async_copy(hp_ref.at[pl.ds(tok, 1)], xe_ref.at[0, pl.ds(s, 1)], sem)

    def start(s, carry):
        row_copy(s).start()
        return carry

    def wait(s, carry):
        row_copy(s).wait()
        return carry

    lax.fori_loop(0, rows, start, 0)
    lax.fori_loop(0, rows, wait, 0)


def _gather(idx, hp, cap, rows):
    half = hp.shape[1]
    n_blocks = cap // rows
    idx3 = idx.reshape(N_EXPERTS * n_blocks, 1, rows)
    return pl.pallas_call(
        functools.partial(_gather_kernel, rows=rows),
        grid=(N_EXPERTS, n_blocks),
        in_specs=[pl.BlockSpec((1, 1, rows), lambda e, r: (e * n_blocks + r, 0, 0), memory_space=pltpu.SMEM),
                  pl.BlockSpec(memory_space=pl.ANY)],
        out_specs=pl.BlockSpec((1, rows, half), lambda e, r: (e, r, 0)),
        out_shape=jax.ShapeDtypeStruct((N_EXPERTS, cap, half), U32),
        scratch_shapes=[pltpu.SemaphoreType.DMA(())],
        compiler_params=_params(("arbitrary", "arbitrary"), 32),
        name="expert_gather",
    )(idx3, hp)


def _unpack(xp):
    lo = pltpu.bitcast(xp << 16, F32).astype(BF16)
    hi = pltpu.bitcast(xp & jnp.uint32(0xFFFF0000), F32).astype(BF16)
    return lo, hi


def _ffn_up_kernel(xe_ref, wg_ref, wu_ref, h_ref, wg_bf, wu_bf):
    @pl.when(pl.program_id(2) == 0)
    def _():
        wg_bf[...] = wg_ref[0].astype(BF16)
        wu_bf[...] = wu_ref[0].astype(BF16)

    lo, hi = _unpack(xe_ref[0])
    half = lo.shape[1]

    def proj(w_bf):
        return (jnp.dot(lo, w_bf[:half, :], preferred_element_type=F32)
                + jnp.dot(hi, w_bf[half:, :], preferred_element_type=F32))

    gate = proj(wg_bf)
    up = proj(wu_bf)
    h_ref[0] = (gate * jax.nn.sigmoid(gate) * up).astype(BF16)


def _ffn_up(xe, w_gate, w_up, tr, tf):
    _, cap, half = xe.shape
    d, f = w_gate.shape[1], w_gate.shape[2]
    return pl.pallas_call(
        _ffn_up_kernel,
        grid=(N_EXPERTS, f // tf, cap // tr),
        in_specs=[pl.BlockSpec((1, tr, half), lambda e, j, r: (e, r, 0)),
                  pl.BlockSpec((1, d, tf), lambda e, j, r: (e, 0, j)),
                  pl.BlockSpec((1, d, tf), lambda e, j, r: (e, 0, j))],
        out_specs=pl.BlockSpec((1, tr, tf), lambda e, j, r: (e, r, j)),
        out_shape=jax.ShapeDtypeStruct((N_EXPERTS, cap, f), BF16),
        scratch_shapes=[pltpu.VMEM((d, tf), BF16), pltpu.VMEM((d, tf), BF16)],
        compiler_params=_params(("parallel", "parallel", "arbitrary"), 56),
        name="expert_ffn_up",
    )(xe, w_gate, w_up)


def _pack_halves(x):
    w = x.shape[1] // 2
    bits = pltpu.bitcast(x.astype(BF16).astype(F32), U32)
    return (bits[:, :w] >> 16) | (bits[:, w:] & jnp.uint32(0xFFFF0000))


def _ffn_down_kernel(h_ref, wd_ref, g_ref, o_ref, wd_bf):
    @pl.when(pl.program_id(2) == 0)
    def _():
        wd_bf[...] = wd_ref[0].astype(BF16)

    out = jnp.dot(h_ref[0], wd_bf[...], preferred_element_type=F32)
    o_ref[0] = _pack_halves(g_ref[0] * out)


def _ffn_down(hid, w_down, gates, tr, tn):
    _, cap, f = hid.shape
    d = w_down.shape[2]
    return pl.pallas_call(
        _ffn_down_kernel,
        grid=(N_EXPERTS, d // tn, cap // tr),
        in_specs=[pl.BlockSpec((1, tr, f), lambda e, j, r: (e, r, 0)),
                  pl.BlockSpec((1, f, tn), lambda e, j, r: (e, 0, j)),
                  pl.BlockSpec((1, tr, 1), lambda e, j, r: (e, r, 0))],
        out_specs=pl.BlockSpec((1, tr, tn // 2), lambda e, j, r: (e, r, j)),
        out_shape=jax.ShapeDtypeStruct((N_EXPERTS, cap, d // 2), U32),
        scratch_shapes=[pltpu.VMEM((f, tn), BF16)],
        compiler_params=_params(("parallel", "parallel", "arbitrary"), 48),
        name="expert_ffn_down",
    )(hid, w_down, gates.reshape(N_EXPERTS, cap, 1))


def _combine_kernel(lo_ref, hi_ref, maxc_ref, idx_ref, x_ref, cnt_ref, con_ref, y_ref, stage_ref, sem,
                    *, nt, cap, tn):
    t = pl.program_id(0)

    @pl.when(t == 0)
    def _():
        stage_ref[...] = jnp.zeros(stage_ref.shape, U32)

    def row_copy(row):
        v = idx_ref[row]
        tok = (v >> RANK_BITS) - t * SEL_T
        return pltpu.make_async_copy(con_ref.at[pl.ds(row, 1)],
                                     stage_ref.at[v & ((1 << RANK_BITS) - 1), pl.ds(tok, 1)], sem)

    def for_rows(fn):
        def per_row(row, carry):
            fn(row_copy(row))
            return carry

        def per_expert(e, carry):
            k = e * nt + t
            return lax.fori_loop(e * cap + lo_ref[k], e * cap + hi_ref[k], per_row, carry)

        lax.fori_loop(0, N_EXPERTS, per_expert, 0)

    for_rows(lambda copy: copy.start())
    y_ref[...] = x_ref[...]
    for_rows(lambda copy: copy.wait())

    half = tn // 2
    cnt = jnp.broadcast_to(cnt_ref[:, 0:1], (SEL_T, half))

    def add_rank(k, carry):
        live = k < cnt
        for j in range(y_ref.shape[1] // tn):
            xp = stage_ref[k, :, j * half:(j + 1) * half]
            lo = pltpu.bitcast(xp << 16, F32)
            hi = pltpu.bitcast(xp & jnp.uint32(0xFFFF0000), F32)
            y_ref[:, j * tn:j * tn + half] += jnp.where(live, lo, 0.0)
            y_ref[:, j * tn + half:(j + 1) * tn] += jnp.where(live, hi, 0.0)
        return carry

    lax.fori_loop(0, maxc_ref[t], add_rank, 0)


def _combine(x1, idx, contrib, offs, tokcnt, cap, tn):
    n_tok, d = x1.shape
    nt = n_tok // SEL_T
    lo = jnp.transpose(offs)
    hi = jnp.concatenate([lo[:, 1:], jnp.full((N_EXPERTS, 1), cap, I32)], axis=1)
    maxc = jnp.max(tokcnt[:, 0].reshape(nt, SEL_T), axis=1)
    grid_spec = pltpu.PrefetchScalarGridSpec(
        num_scalar_prefetch=4,
        grid=(nt,),
        in_specs=[
            pl.BlockSpec((SEL_T, d), lambda t, *_: (t, 0)),
            pl.BlockSpec((SEL_T, N_EXPERTS), lambda t, *_: (t, 0)),
            pl.BlockSpec(memory_space=pl.ANY),
        ],
        out_specs=pl.BlockSpec((SEL_T, d), lambda t, *_: (t, 0)),
        scratch_shapes=[pltpu.VMEM((N_EXPERTS, SEL_T, d // 2), U32), pltpu.SemaphoreType.DMA(())],
    )
    return pl.pallas_call(
        functools.partial(_combine_kernel, nt=nt, cap=cap, tn=tn),
        grid_spec=grid_spec,
        out_shape=jax.ShapeDtypeStruct((n_tok, d), F32),
        compiler_params=_params(("arbitrary",), 40),
        name="expert_combine",
    )(lo.reshape(-1), hi.reshape(-1), maxc, idx.reshape(-1), x1, tokcnt,
      contrib.reshape(N_EXPERTS * cap, d // 2))


DOWN_TN = 1024
UP_TF = 256
OUT_TN = 1024


def _tiles(n_tok, n_seq, cap):
    return dict(
        tm=min(512, n_seq),
        rows=min(512, cap),
        tr=min(1024, cap),
    )


def _encoder_layer(x, p):
    b, n, d = x.shape
    n_tok = b * n
    cap = EC_FACTOR * n_tok // N_EXPERTS
    tl = _tiles(n_tok, n, cap)
    x2 = x.reshape(n_tok, d)

    proj = _in_projection(x2, p["norm1"], p["w_in"], p["gains"], p["cos"][n], p["sin"][n], n, tl["tm"])
    oa = _attention_a(proj, b, n, ATT_TQ)
    ob = _attention_b(proj, p["sink"], p["bias"], b, n)
    x1 = _out_projection(oa, ob, p["out_norm_a"], p["out_norm_b"], p["w_out"], x2, tl["tm"], OUT_TN)

    hp, aff = _router(x1, p["norm2"], p["w_router"], tl["tm"])
    idx, gates, offs, tokcnt = _select(aff, cap)
    xe = _gather(idx, hp, cap, tl["rows"])
    hid = _ffn_up(xe, p["w_gate"], p["w_up"], tl["tr"], UP_TF)
    contrib = _ffn_down(hid, p["w_down"], gates, tl["tr"], DOWN_TN)
    y = _combine(x1, idx, contrib, offs, tokcnt, cap, DOWN_TN)
    return y.reshape(b, n, d)


def kernel(x_prompt, x_sample, norm1, w_in, q_norm_a, k_norm_a, q_norm_b, k_norm_b, sink_b, rel_bias,
           out_norm_a, out_norm_b, w_out, norm2, w_router, w_gate, w_up, w_down):
    depth = norm1.shape[0]
    seqs = sorted({x_prompt.shape[1], x_sample.shape[1]})
    tables = {n: _rope_tables(n) for n in seqs}
    bias = _window_bias(rel_bias)
    ones = jnp.ones((KVW,), F32)

    layers = []
    for l in range(depth):
        gains = jnp.concatenate([jnp.tile(q_norm_a[l], H_A), jnp.tile(k_norm_a[l], KV_A), ones,
                                 jnp.tile(q_norm_b[l], H_B), jnp.tile(k_norm_b[l], KV_B), ones])
        layers.append(dict(
            norm1=norm1[l], w_in=w_in[l].astype(BF16), gains=gains.reshape(1, IN_WIDTH),
            cos={n: tables[n][0] for n in seqs}, sin={n: tables[n][1] for n in seqs},
            sink=sink_b[l], bias=bias,
            out_norm_a=out_norm_a[l], out_norm_b=out_norm_b[l], w_out=w_out[l].astype(BF16),
            norm2=norm2[l], w_router=w_router[l].astype(BF16),
            w_gate=w_gate[l], w_up=w_up[l], w_down=w_down[l],
        ))

    def run(x):
        for p in layers:
            x = _encoder_layer(x, p)
        return x

    return (run(x_prompt), run(x_sample))
```

```python
import functools
import math

import jax
import jax.numpy as jnp
from jax import lax
from jax.experimental import pallas as pl
from jax.experimental.pallas import tpu as pltpu

F32 = jnp.float32
BF16 = jnp.bfloat16
I32 = jnp.int32
U32 = jnp.uint32

HEAD_DIM = 128
H_A = 16
KV_A = 4
H_B = 16
KV_B = 4
GROUP = 4
W_A = H_A * HEAD_DIM
W_B = H_B * HEAD_DIM
KVW = KV_A * HEAD_DIM
IN_WIDTH = W_A + 2 * KVW + W_B + 2 * KVW
GRID_W = 64
HALF_ROT = HEAD_DIM // 2
ROPE_THETA = 10000.0
REL_BUCKETS = 32
REL_MAX_DIST = 128
WINDOW = 128
N_EXPERTS = 16
EC_FACTOR = 2
NORM_EPS = 1e-6
NEG_INF = -1e30
SCALE = 1.0 / math.sqrt(HEAD_DIM)

V7X_VMEM_BYTES = 64 * 1024 * 1024
LANE = 128


def _vmem(mib):
    assert mib * 1024 * 1024 < V7X_VMEM_BYTES
    return mib * 1024 * 1024


def _params(sem, mib):
    return pltpu.CompilerParams(dimension_semantics=sem, vmem_limit_bytes=_vmem(mib))


IN_TN = 512
_ROPE_TILES = 5
_V_TILES = (5, 11)


def _inproj_kernel(x_ref, g1_ref, w_ref, gain_ref, cos_ref, sin_ref, o_ref, h_ref):
    j = pl.program_id(1)

    @pl.when(j == 0)
    def _():
        x = x_ref[...]
        ms = jnp.mean(x * x, axis=-1, keepdims=True)
        h_ref[...] = (x * lax.rsqrt(ms + NORM_EPS) * g1_ref[...]).astype(BF16)

    acc = jnp.dot(h_ref[...], w_ref[...], preferred_element_type=F32)
    is_v = jnp.logical_or(j == _V_TILES[0], j == _V_TILES[1])
    is_rope = j < _ROPE_TILES

    def head_norm():
        outs = []
        for hh in range(IN_TN // HEAD_DIM):
            a = acc[:, hh * HEAD_DIM:(hh + 1) * HEAD_DIM]
            ms = jnp.mean(a * a, axis=-1, keepdims=True)
            outs.append(a * lax.rsqrt(ms + NORM_EPS) * gain_ref[:, hh * HEAD_DIM:(hh + 1) * HEAD_DIM])
        return jnp.concatenate(outs, axis=-1)

    @pl.when(is_v)
    def _():
        o_ref[...] = acc.astype(BF16)

    @pl.when(jnp.logical_and(jnp.logical_not(is_v), jnp.logical_not(is_rope)))
    def _():
        is_qb = j < (IN_WIDTH - 2 * KVW) // IN_TN
        o_ref[...] = (head_norm() * jnp.where(is_qb, SCALE * LOG2E, 1.0)).astype(BF16)

    @pl.when(is_rope)
    def _():
        y = head_norm()
        reps = IN_TN // HEAD_DIM
        c = jnp.concatenate([cos_ref[...]] * reps, axis=-1)
        s = jnp.concatenate([sin_ref[...]] * reps, axis=-1)
        lane = lax.broadcasted_iota(I32, y.shape, 1)
        first_half = (lane % (2 * (HALF_ROT // 2))) < (HALF_ROT // 2)
        partner = jnp.where(first_half,
                            pltpu.roll(y, IN_TN - HALF_ROT // 2, axis=1),
                            pltpu.roll(y, HALF_ROT // 2, axis=1))
        qscale = jnp.where(j < W_A // IN_TN, SCALE * LOG2E, 1.0)
        o_ref[...] = ((y * c + partner * s) * qscale).astype(BF16)


def _rope_tables(n):
    rows = n // GRID_W
    row = jnp.repeat(jnp.arange(rows, dtype=F32), GRID_W)
    col = jnp.tile(jnp.arange(GRID_W, dtype=F32), rows)
    inv = ROPE_THETA ** (-jnp.arange(0, HALF_ROT, 2, dtype=F32) / HALF_ROT)
    ar, ac = row[:, None] * inv, col[:, None] * inv
    cr, sr, cc, sc = jnp.cos(ar), jnp.sin(ar), jnp.cos(ac), jnp.sin(ac)
    cos_t = jnp.concatenate([cr, cr, cc, cc], axis=-1)
    sin_t = jnp.concatenate([-sr, sr, -sc, sc], axis=-1)
    return cos_t, sin_t


def _in_projection(x2, norm1, w_in_bf, gains, cos_t, sin_t, n_seq, tm):
    n_tok, d = x2.shape
    seq_blocks = n_seq // tm
    return pl.pallas_call(
        _inproj_kernel,
        grid=(n_tok // tm, IN_WIDTH // IN_TN),
        in_specs=[
            pl.BlockSpec((tm, d), lambda i, j: (i, 0)),
            pl.BlockSpec((1, d), lambda i, j: (0, 0)),
            pl.BlockSpec((d, IN_TN), lambda i, j: (0, j)),
            pl.BlockSpec((1, IN_TN), lambda i, j: (0, j)),
            pl.BlockSpec((tm, HEAD_DIM), lambda i, j: (i % seq_blocks, 0)),
            pl.BlockSpec((tm, HEAD_DIM), lambda i, j: (i % seq_blocks, 0)),
        ],
        out_specs=pl.BlockSpec((tm, IN_TN), lambda i, j: (i, j)),
        out_shape=jax.ShapeDtypeStruct((n_tok, IN_WIDTH), BF16),
        scratch_shapes=[pltpu.VMEM((tm, d), BF16)],
        compiler_params=_params(("parallel", "arbitrary"), 48),
        name="in_projection",
    )(x2, norm1.reshape(1, d), w_in_bf, gains, cos_t, sin_t)


ATT_TQ = 128
ATT_TKC = 256
ATT_UNROLL = 32
LOG2E = 1.4426950408889634


def _flash_kernel(q_ref, k_ref, v_ref, o_ref, vt_ref, *, n_chunks, unroll):
    @pl.when(pl.program_id(2) == 0)
    def _():
        def transpose_chunk(c, carry):
            st = pl.multiple_of(c * ATT_TKC, ATT_TKC)
            vt_ref[:, pl.ds(st, ATT_TKC)] = v_ref[pl.ds(st, ATT_TKC), :].T
            return carry
        lax.fori_loop(0, n_chunks, transpose_chunk, 0)

    q = q_ref[...]
    tq = q.shape[0]
    q_t = jnp.concatenate([q[:, g * HEAD_DIM:(g + 1) * HEAD_DIM].T for g in range(GROUP)], axis=1)
    nq = GROUP * tq

    def body(c, carry):
        m, l, acc = carry
        starts = [pl.multiple_of((c * unroll + u) * ATT_TKC, ATT_TKC) for u in range(unroll)]
        scores = [jnp.dot(k_ref[pl.ds(st, ATT_TKC), :], q_t, preferred_element_type=F32)
                  for st in starts]
        for st, s in zip(starts, scores):
            m_new = jnp.maximum(m, jnp.max(s, axis=0, keepdims=True))
            alpha = jnp.exp2(m - m_new)
            p = jnp.exp2(s - m_new)
            l = alpha * l + jnp.sum(p, axis=0, keepdims=True)
            pv = jnp.dot(vt_ref[:, pl.ds(st, ATT_TKC)], p.astype(BF16), preferred_element_type=F32)
            acc = alpha * acc + pv
            m = m_new
        return m, l, acc

    init = (jnp.full((1, nq), -jnp.inf, F32), jnp.zeros((1, nq), F32), jnp.zeros((HEAD_DIM, nq), F32))
    _, l, acc = lax.fori_loop(0, n_chunks // unroll, body, init)
    o_t = acc / l
    for g in range(GROUP):
        o_ref[:, g * HEAD_DIM:(g + 1) * HEAD_DIM] = o_t[:, g * tq:(g + 1) * tq].T


def _attention_a(proj, b, n, tq):
    n_tok = b * n
    qb = n // tq
    k_col = W_A // HEAD_DIM
    v_col = (W_A + KVW) // HEAD_DIM
    n_chunks = n // ATT_TKC
    unroll = math.gcd(n_chunks, ATT_UNROLL)
    return pl.pallas_call(
        functools.partial(_flash_kernel, n_chunks=n_chunks, unroll=unroll),
        grid=(b, KV_A, qb),
        in_specs=[
            pl.BlockSpec((tq, GROUP * HEAD_DIM), lambda bi, h, qi: (bi * qb + qi, h)),
            pl.BlockSpec((n, HEAD_DIM), lambda bi, h, qi: (bi, k_col + h)),
            pl.BlockSpec((n, HEAD_DIM), lambda bi, h, qi: (bi, v_col + h)),
        ],
        out_specs=pl.BlockSpec((tq, GROUP * HEAD_DIM), lambda bi, h, qi: (bi * qb + qi, h)),
        out_shape=jax.ShapeDtypeStruct((n_tok, W_A), F32),
        scratch_shapes=[pltpu.VMEM((HEAD_DIM, n), BF16)],
        compiler_params=_params(("arbitrary", "arbitrary", "arbitrary"), 48),
        name="attention_global",
    )(proj, proj, proj)


WIN_TQ = 2 * WINDOW
WIN_TK = WIN_TQ + 2 * WINDOW


def _t5_buckets(rel):
    nb = REL_BUCKETS // 2
    max_exact = nb // 2
    ret = jnp.where(rel > 0, nb, 0)
    n = jnp.abs(rel)
    nf = jnp.maximum(n, 1).astype(F32)
    large = max_exact + (jnp.log(nf / max_exact) / math.log(REL_MAX_DIST / max_exact)
                         * (nb - max_exact)).astype(I32)
    large = jnp.minimum(large, nb - 1)
    return ret + jnp.where(n < max_exact, n, large)


def _window_bias(rel_bias):
    span = WIN_TQ + WIN_TK
    kk = jnp.arange(span)
    rel = jnp.where(kk < WIN_TK, kk, kk - span) - WINDOW
    tbl = jnp.where((jnp.abs(rel) <= WINDOW)[:, None], rel_bias[_t5_buckets(rel)].astype(F32), NEG_INF)
    flat = jnp.tile(tbl.T, (1, WIN_TQ))[:, :WIN_TQ * (span - 1)]
    bias = flat.reshape(H_B, WIN_TQ, span - 1)[:, :, :WIN_TK]
    j = jnp.arange(WIN_TK)
    before, after = j < WINDOW, j >= WINDOW + WIN_TQ
    masks = jnp.stack([jnp.zeros_like(before), before, after, jnp.logical_or(before, after)])
    bias = jnp.where(masks[:, None, None, :], NEG_INF, bias[None])
    bias = bias.reshape(4, KV_B, GROUP, WIN_TQ, WIN_TK).transpose(0, 1, 4, 2, 3)
    return bias.reshape(4, KV_B, WIN_TK, GROUP * WIN_TQ) * LOG2E


def _window_kernel(sink_ref, q_ref, kp_ref, kc_ref, kn_ref, vp_ref, vc_ref, vn_ref, bias_ref, o_ref):
    h = pl.program_id(0)
    q = q_ref[...]
    q_all = jnp.concatenate([q[:, g * HEAD_DIM:(g + 1) * HEAD_DIM] for g in range(GROUP)], axis=0)
    kcat = jnp.concatenate([kp_ref[...], kc_ref[...], kn_ref[...]], axis=0)
    vcat = jnp.concatenate([vp_ref[...], vc_ref[...], vn_ref[...]], axis=0)
    s = lax.dot_general(kcat, q_all, (((1,), (1,)), ((), ())), preferred_element_type=F32) + bias_ref[0, 0]
    head = lax.broadcasted_iota(I32, (1, GROUP * WIN_TQ), 1) // WIN_TQ
    snk = jnp.zeros((1, GROUP * WIN_TQ), F32)
    for g in range(GROUP):
        snk = jnp.where(head == g, sink_ref[h * GROUP + g] * LOG2E, snk)
    m = jnp.maximum(jnp.max(s, axis=0, keepdims=True), snk)
    e = jnp.exp2(s - m)
    denom = jnp.sum(e, axis=0, keepdims=True) + jnp.exp2(snk - m)
    o_t = lax.dot_general(vcat, e.astype(BF16), (((0,), (0,)), ((), ())), preferred_element_type=F32) / denom
    for g in range(GROUP):
        o_ref[:, g * HEAD_DIM:(g + 1) * HEAD_DIM] = o_t[:, g * WIN_TQ:(g + 1) * WIN_TQ].T


def _attention_b(proj, sink, bias, b, n):
    n_tok = b * n
    qb = n // WIN_TQ
    nb128 = n // WINDOW
    base = W_A + 2 * KVW
    q_col = base // (GROUP * HEAD_DIM)
    k_col = (base + W_B) // HEAD_DIM
    v_col = (base + W_B + KVW) // HEAD_DIM

    def prev_map(col):
        return lambda h, bi, i, s: (bi * nb128 + jnp.maximum(2 * i - 1, 0), col + h)

    def cur_map(col):
        return lambda h, bi, i, s: (bi * qb + i, col + h)

    def next_map(col):
        return lambda h, bi, i, s: (bi * nb128 + jnp.minimum(2 * i + 2, nb128 - 1), col + h)

    grid_spec = pltpu.PrefetchScalarGridSpec(
        num_scalar_prefetch=1,
        grid=(KV_B, b, qb),
        in_specs=[
            pl.BlockSpec((WIN_TQ, GROUP * HEAD_DIM), lambda h, bi, i, s: (bi * qb + i, q_col + h)),
            pl.BlockSpec((WINDOW, HEAD_DIM), prev_map(k_col)),
            pl.BlockSpec((WIN_TQ, HEAD_DIM), cur_map(k_col)),
            pl.BlockSpec((WINDOW, HEAD_DIM), next_map(k_col)),
            pl.BlockSpec((WINDOW, HEAD_DIM), prev_map(v_col)),
            pl.BlockSpec((WIN_TQ, HEAD_DIM), cur_map(v_col)),
            pl.BlockSpec((WINDOW, HEAD_DIM), next_map(v_col)),
            pl.BlockSpec((1, 1, WIN_TK, GROUP * WIN_TQ),
                         lambda h, bi, i, s: (jnp.where(i == 0, 1, 0) + jnp.where(i == qb - 1, 2, 0), h, 0, 0)),
        ],
        out_specs=pl.BlockSpec((WIN_TQ, GROUP * HEAD_DIM), lambda h, bi, i, s: (bi * qb + i, h)),
    )
    return pl.pallas_call(
        _window_kernel,
        grid_spec=grid_spec,
        out_shape=jax.ShapeDtypeStruct((n_tok, W_B), F32),
        compiler_params=_params(("parallel", "parallel", "parallel"), 32),
        name="attention_window",
    )(sink, proj, proj, proj, proj, proj, proj, proj, bias)


def _outproj_kernel(oa_ref, ob_ref, ga_ref, gb_ref, w_ref, x_ref, o_ref, mix_ref):
    j = pl.program_id(1)

    @pl.when(j == 0)
    def _():
        for src, gain, lo in ((oa_ref, ga_ref, 0), (ob_ref, gb_ref, W_A)):
            o = src[...]
            ms = jnp.mean(o * o, axis=-1, keepdims=True)
            mix_ref[:, lo:lo + o.shape[1]] = (o * lax.rsqrt(ms + NORM_EPS) * gain[...]).astype(BF16)

    o_ref[...] = x_ref[...] + jnp.dot(mix_ref[...], w_ref[...], preferred_element_type=F32)


def _out_projection(oa, ob, gain_a, gain_b, w_out_bf, x2, tm, tn):
    n_tok, d = x2.shape
    return pl.pallas_call(
        _outproj_kernel,
        grid=(n_tok // tm, d // tn),
        in_specs=[
            pl.BlockSpec((tm, W_A), lambda i, j: (i, 0)),
            pl.BlockSpec((tm, W_B), lambda i, j: (i, 0)),
            pl.BlockSpec((1, W_A), lambda i, j: (0, 0)),
            pl.BlockSpec((1, W_B), lambda i, j: (0, 0)),
            pl.BlockSpec((W_A + W_B, tn), lambda i, j: (0, j)),
            pl.BlockSpec((tm, tn), lambda i, j: (i, j)),
        ],
        out_specs=pl.BlockSpec((tm, tn), lambda i, j: (i, j)),
        out_shape=jax.ShapeDtypeStruct((n_tok, d), F32),
        scratch_shapes=[pltpu.VMEM((tm, W_A + W_B), BF16)],
        compiler_params=_params(("parallel", "arbitrary"), 48),
        name="out_projection",
    )(oa, ob, gain_a.reshape(1, W_A), gain_b.reshape(1, W_B), w_out_bf, x2)


def _router_kernel(x_ref, g_ref, wr_ref, hp_ref, aff_ref):
    x = x_ref[...]
    ms = jnp.mean(x * x, axis=-1, keepdims=True)
    hb = (x * lax.rsqrt(ms + NORM_EPS) * g_ref[...]).astype(BF16)
    logits = jnp.dot(hb, wr_ref[...], preferred_element_type=F32)
    e = jnp.exp(logits - jnp.max(logits, axis=-1, keepdims=True))
    aff_ref[...] = e / jnp.sum(e, axis=-1, keepdims=True)
    half = hb.shape[1] // 2
    bits = pltpu.bitcast(hb.astype(F32), U32)
    hp_ref[...] = (bits[:, :half] >> 16) | (bits[:, half:] & jnp.uint32(0xFFFF0000))


def _router(x1, norm2, w_router_bf, tm):
    n_tok, d = x1.shape
    return pl.pallas_call(
        _router_kernel,
        grid=(n_tok // tm,),
        in_specs=[
            pl.BlockSpec((tm, d), lambda i: (i, 0)),
            pl.BlockSpec((1, d), lambda i: (0, 0)),
            pl.BlockSpec((d, N_EXPERTS), lambda i: (0, 0)),
        ],
        out_specs=[pl.BlockSpec((tm, d // 2), lambda i: (i, 0)),
                   pl.BlockSpec((tm, N_EXPERTS), lambda i: (i, 0))],
        out_shape=[jax.ShapeDtypeStruct((n_tok, d // 2), U32),
                   jax.ShapeDtypeStruct((n_tok, N_EXPERTS), F32)],
        compiler_params=_params(("parallel",), 48),
        name="router",
    )(x1, norm2.reshape(1, d), w_router_bf)


SEL_T = 128
SEL_CHUNK = 1024
RANK_BITS = 4
assert N_EXPERTS <= 1 << RANK_BITS


def _slots_kernel(aff_ref, slot_ref, tokrank_ref, tokcnt_ref, off_ref, thr_ref, need_ref, carry_ref, *, cap, n_tok):
    t = pl.program_id(0)
    n_chunks = n_tok // SEL_CHUNK

    def count(pred_fn):
        def body(c, tot):
            bits = pltpu.bitcast(aff_ref[pl.ds(c * SEL_CHUNK, SEL_CHUNK), :], I32)
            return tot + jnp.sum(pred_fn(bits).astype(I32), axis=0, keepdims=True)
        return lax.fori_loop(0, n_chunks, body, jnp.zeros((1, N_EXPERTS), I32))

    @pl.when(t == 0)
    def _():
        def bit_body(b, thr):
            cand = thr | jnp.left_shift(jnp.int32(1), 30 - b)
            return jnp.where(count(lambda bits: bits >= cand) >= cap, cand, thr)
        thr = lax.fori_loop(0, 31, bit_body, jnp.zeros((1, N_EXPERTS), I32))
        thr_ref[...] = thr
        need_ref[...] = (cap - count(lambda bits: bits > thr)).astype(F32)
        carry_ref[...] = jnp.zeros(carry_ref.shape, F32)

    a = aff_ref[pl.ds(t * SEL_T, SEL_T), :]
    bits = pltpu.bitcast(a, I32)
    thr = thr_ref[...]
    gt = bits > thr
    eq = bits == thr
    r = lax.broadcasted_iota(I32, (SEL_T, SEL_T), 0)
    c = lax.broadcasted_iota(I32, (SEL_T, SEL_T), 1)
    tri = jnp.where(r >= c, 1.0, 0.0).astype(BF16)
    eq_f = jnp.where(eq, 1.0, 0.0)
    cum_eq = jnp.dot(tri, eq_f.astype(BF16), preferred_element_type=F32) + carry_ref[0:1, :]
    sel = jnp.logical_or(gt, jnp.logical_and(eq, cum_eq <= need_ref[...]))
    sel_f = jnp.where(sel, 1.0, 0.0)
    pos = jnp.dot(tri, sel_f.astype(BF16), preferred_element_type=F32) + carry_ref[1:2, :]
    off_ref[0] = carry_ref[1:2, :].astype(I32)
    carry_ref[0:1, :] = carry_ref[0:1, :] + jnp.sum(eq_f, axis=0, keepdims=True)
    carry_ref[1:2, :] = carry_ref[1:2, :] + jnp.sum(sel_f, axis=0, keepdims=True)
    slot_ref[...] = jnp.where(sel, pos - 1.0, -1.0).astype(I32)
    er = lax.broadcasted_iota(I32, (N_EXPERTS, N_EXPERTS), 0)
    ec = lax.broadcasted_iota(I32, (N_EXPERTS, N_EXPERTS), 1)
    before = jnp.where(er < ec, 1.0, 0.0).astype(BF16)
    rank = jnp.dot(sel_f.astype(BF16), before, preferred_element_type=F32).astype(I32)
    tok = t * SEL_T + lax.broadcasted_iota(I32, (SEL_T, N_EXPERTS), 0)
    tokrank_ref[...] = tok * (1 << RANK_BITS) + rank
    tokcnt_ref[...] = jnp.broadcast_to(jnp.sum(sel_f, axis=1, keepdims=True).astype(I32), (SEL_T, N_EXPERTS))


def _lists_kernel(off_ref, slot_ref, tokrank_ref, aff_ref, idx_ref, gate_ref):
    t = pl.program_id(0)

    @pl.when(t == 0)
    def _():
        idx_ref[...] = jnp.zeros(idx_ref.shape, I32)
        gate_ref[...] = jnp.zeros(gate_ref.shape, F32)

    slot = slot_ref[...]
    tokrank = tokrank_ref[...]
    a = aff_ref[...]
    for e in range(N_EXPERTS):
        base = pl.multiple_of((off_ref[t * N_EXPERTS + e] // LANE) * LANE, LANE)
        lane = base + lax.broadcasted_iota(I32, (1, 2 * LANE), 1)
        match = slot[:, e:e + 1] == lane
        win = (slice(e, e + 1), pl.ds(base, 2 * LANE))
        idx_ref[win] += jnp.sum(jnp.where(match, tokrank[:, e:e + 1], 0), axis=0, keepdims=True)
        gate_ref[win] += jnp.sum(jnp.where(match, a[:, e:e + 1], 0.0), axis=0, keepdims=True)


def _select(aff, cap):
    n_tok = aff.shape[0]
    nt = n_tok // SEL_T
    tile = pl.BlockSpec((SEL_T, N_EXPERTS), lambda t: (t, 0))
    tile_shape = jax.ShapeDtypeStruct((n_tok, N_EXPERTS), I32)
    slot, tokrank, tokcnt, offs = pl.pallas_call(
        functools.partial(_slots_kernel, cap=cap, n_tok=n_tok),
        grid=(nt,),
        in_specs=[pl.BlockSpec((n_tok, N_EXPERTS), lambda t: (0, 0))],
        out_specs=[tile, tile, tile, pl.BlockSpec((1, 1, N_EXPERTS), lambda t: (t, 0, 0))],
        out_shape=[tile_shape, tile_shape, tile_shape, jax.ShapeDtypeStruct((nt, 1, N_EXPERTS), I32)],
        scratch_shapes=[pltpu.VMEM((1, N_EXPERTS), I32), pltpu.VMEM((1, N_EXPERTS), F32),
                        pltpu.VMEM((2, N_EXPERTS), F32)],
        compiler_params=_params(("arbitrary",), 32),
        name="expert_slots",
    )(aff)
    offs = offs.reshape(nt, N_EXPERTS)
    width = cap + 2 * LANE
    tile_p = pl.BlockSpec((SEL_T, N_EXPERTS), lambda t, o: (t, 0))
    idx, gates = pl.pallas_call(
        _lists_kernel,
        grid_spec=pltpu.PrefetchScalarGridSpec(
            num_scalar_prefetch=1, grid=(nt,),
            in_specs=[tile_p, tile_p, tile_p],
            out_specs=[pl.BlockSpec((N_EXPERTS, width), lambda t, o: (0, 0)),
                       pl.BlockSpec((N_EXPERTS, width), lambda t, o: (0, 0))]),
        out_shape=[jax.ShapeDtypeStruct((N_EXPERTS, width), I32),
                   jax.ShapeDtypeStruct((N_EXPERTS, width), F32)],
        compiler_params=_params(("arbitrary",), 32),
        name="expert_lists",
    )(offs.reshape(-1), slot, tokrank, aff)
    return idx[:, :cap], gates[:, :cap], offs, tokcnt


def _gather_kernel(idx_ref, hp_ref, xe_ref, sem, *, rows):
    def row_copy(s):
        tok = idx_ref[0, 0, s] >> RANK_BITS
        return pltpu.make_async_copy(hp_ref.at[pl.ds(tok, 1)], xe_ref.at[0, pl.ds(s, 1)], sem)

    def start(s, carry):
        row_copy(s).start()
        return carry

    def wait(s, carry):
        row_copy(s).wait()
        return carry

    lax.fori_loop(0, rows, start, 0, unroll=GATHER_UNROLL)
    lax.fori_loop(0, rows, wait, 0, unroll=GATHER_UNROLL)


GATHER_UNROLL = 8


def _gather(idx, hp, cap, rows):
    half = hp.shape[1]
    n_blocks = cap // rows
    idx3 = idx.reshape(N_EXPERTS * n_blocks, 1, rows)
    return pl.pallas_call(
        functools.partial(_gather_kernel, rows=rows),
        grid=(N_EXPERTS, n_blocks),
        in_specs=[pl.BlockSpec((1, 1, rows), lambda e, r: (e * n_blocks + r, 0, 0), memory_space=pltpu.SMEM),
                  pl.BlockSpec(memory_space=pl.ANY)],
        out_specs=pl.BlockSpec((1, rows, half), lambda e, r: (e, r, 0)),
        out_shape=jax.ShapeDtypeStruct((N_EXPERTS, cap, half), U32),
        scratch_shapes=[pltpu.SemaphoreType.DMA(())],
        compiler_params=_params(("arbitrary", "arbitrary"), 32),
        name="expert_gather",
    )(idx3, hp)


def _unpack(xp):
    lo = pltpu.bitcast(xp << 16, F32).astype(BF16)
    hi = pltpu.bitcast(xp & jnp.uint32(0xFFFF0000), F32).astype(BF16)
    return lo, hi


def _ffn_up_kernel(xe_hbm, wg_ref, wu_ref, h_ref, xe_vmem, wg_bf, wu_bf, sem, *, tr):
    e, j, r = pl.program_id(0), pl.program_id(1), pl.program_id(2)

    @pl.when(jnp.logical_and(j == 0, r == 0))
    def _():
        copy = pltpu.make_async_copy(xe_hbm.at[e], xe_vmem, sem)
        copy.start()
        copy.wait()

    @pl.when(r == 0)
    def _():
        wg_bf[...] = wg_ref[0].astype(BF16)
        wu_bf[...] = wu_ref[0].astype(BF16)

    lo, hi = _unpack(xe_vmem[pl.ds(pl.multiple_of(r * tr, tr), tr), :])
    half = lo.shape[1]

    def proj(w_bf):
        return (jnp.dot(lo, w_bf[:half, :], preferred_element_type=F32)
                + jnp.dot(hi, w_bf[half:, :], preferred_element_type=F32))

    gate = proj(wg_bf)
    up = proj(wu_bf)
    h_ref[0] = (gate * jax.nn.sigmoid(gate) * up).astype(BF16)


def _ffn_up(xe, w_gate, w_up, tr, tf):
    _, cap, half = xe.shape
    d, f = w_gate.shape[1], w_gate.shape[2]
    return pl.pallas_call(
        functools.partial(_ffn_up_kernel, tr=tr),
        grid=(N_EXPERTS, f // tf, cap // tr),
        in_specs=[pl.BlockSpec(memory_space=pl.ANY),
                  pl.BlockSpec((1, d, tf), lambda e, j, r: (e, 0, j)),
                  pl.BlockSpec((1, d, tf), lambda e, j, r: (e, 0, j))],
        out_specs=pl.BlockSpec((1, tr, tf), lambda e, j, r: (e, r, j)),
        out_shape=jax.ShapeDtypeStruct((N_EXPERTS, cap, f), BF16),
        scratch_shapes=[pltpu.VMEM((cap, half), U32), pltpu.VMEM((d, tf), BF16), pltpu.VMEM((d, tf), BF16),
                        pltpu.SemaphoreType.DMA(())],
        compiler_params=_params(("arbitrary", "arbitrary", "arbitrary"), 56),
        name="expert_ffn_up",
    )(xe, w_gate, w_up)


def _pack_halves(x):
    w = x.shape[1] // 2
    bits = pltpu.bitcast(x.astype(BF16).astype(F32), U32)
    return (bits[:, :w] >> 16) | (bits[:, w:] & jnp.uint32(0xFFFF0000))


def _ffn_down_kernel(h_ref, wd_ref, g_ref, o_ref, wd_bf, *, tr):
    @pl.when(pl.program_id(2) == 0)
    def _():
        wd_bf[...] = wd_ref[0].astype(BF16)

    rows = pl.ds(pl.multiple_of(pl.program_id(2) * tr, tr), tr)
    out = jnp.dot(h_ref[0, rows, :], wd_bf[...], preferred_element_type=F32)
    o_ref[0] = _pack_halves(g_ref[0] * out)


def _ffn_down(hid, w_down, gates, tr, tn):
    _, cap, f = hid.shape
    d = w_down.shape[2]
    return pl.pallas_call(
        functools.partial(_ffn_down_kernel, tr=tr),
        grid=(N_EXPERTS, d // tn, cap // tr),
        in_specs=[pl.BlockSpec((1, cap, f), lambda e, j, r: (e, 0, 0)),
                  pl.BlockSpec((1, f, tn), lambda e, j, r: (e, 0, j)),
                  pl.BlockSpec((1, tr, 1), lambda e, j, r: (e, r, 0))],
        out_specs=pl.BlockSpec((1, tr, tn // 2), lambda e, j, r: (e, r, j)),
        out_shape=jax.ShapeDtypeStruct((N_EXPERTS, cap, d // 2), U32),
        scratch_shapes=[pltpu.VMEM((f, tn), BF16)],
        compiler_params=_params(("parallel", "parallel", "arbitrary"), 48),
        name="expert_ffn_down",
    )(hid, w_down, gates.reshape(N_EXPERTS, cap, 1))


def _combine_kernel(lo_ref, hi_ref, maxc_ref, idx_ref, x_ref, cnt_ref, con_ref, y_ref, stage_ref, sem,
                    *, nt, cap, tn):
    t = pl.program_id(0)

    @pl.when(t == 0)
    def _():
        stage_ref[...] = jnp.zeros(stage_ref.shape, U32)

    def row_copy(row):
        v = idx_ref[row]
        tok = (v >> RANK_BITS) - t * SEL_T
        return pltpu.make_async_copy(con_ref.at[pl.ds(row, 1)],
                                     stage_ref.at[v & ((1 << RANK_BITS) - 1), pl.ds(tok, 1)], sem)

    def for_rows(fn):
        def per_row(row, carry):
            fn(row_copy(row))
            return carry

        def per_expert(e, carry):
            k = e * nt + t
            return lax.fori_loop(e * cap + lo_ref[k], e * cap + hi_ref[k], per_row, carry)

        lax.fori_loop(0, N_EXPERTS, per_expert, 0)

    for_rows(lambda copy: copy.start())
    y_ref[...] = x_ref[...]
    for_rows(lambda copy: copy.wait())

    half = tn // 2
    cnt = jnp.broadcast_to(cnt_ref[:, 0:1], (SEL_T, half))

    def add_rank(k, carry):
        live = k < cnt
        for j in range(y_ref.shape[1] // tn):
            xp = stage_ref[k, :, j * half:(j + 1) * half]
            lo = pltpu.bitcast(xp << 16, F32)
            hi = pltpu.bitcast(xp & jnp.uint32(0xFFFF0000), F32)
            y_ref[:, j * tn:j * tn + half] += jnp.where(live, lo, 0.0)
            y_ref[:, j * tn + half:(j + 1) * tn] += jnp.where(live, hi, 0.0)
        return carry

    lax.fori_loop(0, maxc_ref[t], add_rank, 0)


def _combine(x1, idx, contrib, offs, tokcnt, cap, tn):
    n_tok, d = x1.shape
    nt = n_tok // SEL_T
    lo = jnp.transpose(offs)
    hi = jnp.concatenate([lo[:, 1:], jnp.full((N_EXPERTS, 1), cap, I32)], axis=1)
    maxc = jnp.max(tokcnt[:, 0].reshape(nt, SEL_T), axis=1)
    grid_spec = pltpu.PrefetchScalarGridSpec(
        num_scalar_prefetch=4,
        grid=(nt,),
        in_specs=[
            pl.BlockSpec((SEL_T, d), lambda t, *_: (t, 0)),
            pl.BlockSpec((SEL_T, N_EXPERTS), lambda t, *_: (t, 0)),
            pl.BlockSpec(memory_space=pl.ANY),
        ],
        out_specs=pl.BlockSpec((SEL_T, d), lambda t, *_: (t, 0)),
        scratch_shapes=[pltpu.VMEM((N_EXPERTS, SEL_T, d // 2), U32), pltpu.SemaphoreType.DMA(())],
    )
    return pl.pallas_call(
        functools.partial(_combine_kernel, nt=nt, cap=cap, tn=tn),
        grid_spec=grid_spec,
        out_shape=jax.ShapeDtypeStruct((n_tok, d), F32),
        compiler_params=_params(("arbitrary",), 40),
        name="expert_combine",
    )(lo.reshape(-1), hi.reshape(-1), maxc, idx.reshape(-1), x1, tokcnt,
      contrib.reshape(N_EXPERTS * cap, d // 2))


DOWN_TN = 1024
UP_TF = 256
OUT_TN = 1024


def _tiles(n_tok, n_seq, cap):
    return dict(
        tm=min(512, n_seq),
        rows=min(512, cap),
        tr=min(1024, cap),
    )


def _encoder_layer(x, p):
    b, n, d = x.shape
    n_tok = b * n
    cap = EC_FACTOR * n_tok // N_EXPERTS
    tl = _tiles(n_tok, n, cap)
    x2 = x.reshape(n_tok, d)

    proj = _in_projection(x2, p["norm1"], p["w_in"], p["gains"], p["cos"][n], p["sin"][n], n, tl["tm"])
    oa = _attention_a(proj, b, n, ATT_TQ)
    ob = _attention_b(proj, p["sink"], p["bias"], b, n)
    x1 = _out_projection(oa, ob, p["out_norm_a"], p["out_norm_b"], p["w_out"], x2, tl["tm"], OUT_TN)

    hp, aff = _router(x1, p["norm2"], p["w_router"], tl["tm"])
    idx, gates, offs, tokcnt = _select(aff, cap)
    xe = _gather(idx, hp, cap, tl["rows"])
    hid = _ffn_up(xe, p["w_gate"], p["w_up"], tl["tr"], UP_TF)
    contrib = _ffn_down(hid, p["w_down"], gates, tl["tr"], DOWN_TN)
    y = _combine(x1, idx, contrib, offs, tokcnt, cap, DOWN_TN)
    return y.reshape(b, n, d)


def kernel(x_prompt, x_sample, norm1, w_in, q_norm_a, k_norm_a, q_norm_b, k_norm_b, sink_b, rel_bias,
           out_norm_a, out_norm_b, w_out, norm2, w_router, w_gate, w_up, w_down):
    depth = norm1.shape[0]
    seqs = sorted({x_prompt.shape[1], x_sample.shape[1]})
    tables = {n: _rope_tables(n) for n in seqs}
    bias = _window_bias(rel_bias)
    ones = jnp.ones((KVW,), F32)

    layers = []
    for l in range(depth):
        gains = jnp.concatenate([jnp.tile(q_norm_a[l], H_A), jnp.tile(k_norm_a[l], KV_A), ones,
                                 jnp.tile(q_norm_b[l], H_B), jnp.tile(k_norm_b[l], KV_B), ones])
        layers.append(dict(
            norm1=norm1[l], w_in=w_in[l].astype(BF16), gains=gains.reshape(1, IN_WIDTH),
            cos={n: tables[n][0] for n in seqs}, sin={n: tables[n][1] for n in seqs},
            sink=sink_b[l], bias=bias,
            out_norm_a=out_norm_a[l], out_norm_b=out_norm_b[l], w_out=w_out[l].astype(BF16),
            norm2=norm2[l], w_router=w_router[l].astype(BF16),
            w_gate=w_gate[l], w_up=w_up[l], w_down=w_down[l],
        ))

    def run(x):
        for p in layers:
            x = _encoder_layer(x, p)
        return x

    return (run(x_prompt), run(x_sample))
```

```python
import functools
import math

import jax
import jax.numpy as jnp
from jax import lax
from jax.experimental import pallas as pl
from jax.experimental.pallas import tpu as pltpu

F32 = jnp.float32
BF16 = jnp.bfloat16
I32 = jnp.int32
U32 = jnp.uint32

HEAD_DIM = 128
H_A = 16
KV_A = 4
H_B = 16
KV_B = 4
GROUP = 4
W_A = H_A * HEAD_DIM
W_B = H_B * HEAD_DIM
KVW = KV_A * HEAD_DIM
IN_WIDTH = W_A + 2 * KVW + W_B + 2 * KVW
GRID_W = 64
HALF_ROT = HEAD_DIM // 2
ROPE_THETA = 10000.0
REL_BUCKETS = 32
REL_MAX_DIST = 128
WINDOW = 128
N_EXPERTS = 16
EC_FACTOR = 2
NORM_EPS = 1e-6
NEG_INF = -1e30
SCALE = 1.0 / math.sqrt(HEAD_DIM)

V7X_VMEM_BYTES = 64 * 1024 * 1024
LANE = 128


def _vmem(mib):
    assert mib * 1024 * 1024 < V7X_VMEM_BYTES
    return mib * 1024 * 1024


def _params(sem, mib):
    return pltpu.CompilerParams(dimension_semantics=sem, vmem_limit_bytes=_vmem(mib))


IN_TN = 512
_ROPE_TILES = 5
_VA_TILE = 5
_N_TILES = IN_WIDTH // IN_TN
_MAIN_TILES = _N_TILES - 1


def _inproj_kernel(x_ref, g1_ref, w_ref, gain_ref, cos_ref, sin_ref, o_ref, vb_ref, h_ref, raw_ref):
    j = pl.program_id(1)

    @pl.when(j == 0)
    def _():
        x = x_ref[...]
        ms = jnp.mean(x * x, axis=-1, keepdims=True)
        h_ref[...] = (x * lax.rsqrt(ms + NORM_EPS) * g1_ref[...]).astype(BF16)
        raw_ref[...] = jnp.zeros(raw_ref.shape, F32)

    prev = raw_ref[...]
    acc = jnp.dot(h_ref[...], w_ref[...], preferred_element_type=F32)
    jp = j - 1
    is_v = jp == _VA_TILE
    is_rope = jp < _ROPE_TILES
    is_q = jnp.logical_or(jp < W_A // IN_TN,
                          jnp.logical_and(jp > _VA_TILE, jp < (IN_WIDTH - 2 * KVW) // IN_TN))
    outs = []
    for hh in range(IN_TN // HEAD_DIM):
        a = prev[:, hh * HEAD_DIM:(hh + 1) * HEAD_DIM]
        ms = jnp.mean(a * a, axis=-1, keepdims=True)
        r = jnp.where(is_v, 1.0, lax.rsqrt(ms + NORM_EPS))
        outs.append(a * r * gain_ref[:, hh * HEAD_DIM:(hh + 1) * HEAD_DIM])
    y = jnp.concatenate(outs, axis=-1)
    reps = IN_TN // HEAD_DIM
    c = jnp.concatenate([jnp.where(is_rope, cos_ref[...], 1.0)] * reps, axis=-1)
    s = jnp.concatenate([jnp.where(is_rope, sin_ref[...], 0.0)] * reps, axis=-1)
    lane = lax.broadcasted_iota(I32, y.shape, 1)
    first_half = (lane % (2 * (HALF_ROT // 2))) < (HALF_ROT // 2)
    partner = jnp.where(first_half,
                        pltpu.roll(y, IN_TN - HALF_ROT // 2, axis=1),
                        pltpu.roll(y, HALF_ROT // 2, axis=1))
    o_ref[...] = ((y * c + partner * s) * jnp.where(is_q, SCALE * LOG2E, 1.0)).astype(BF16)
    raw_ref[...] = acc

    @pl.when(j == _N_TILES - 1)
    def _():
        vb_ref[...] = acc.astype(BF16)


def _rope_tables(n):
    rows = n // GRID_W
    row = jnp.repeat(jnp.arange(rows, dtype=F32), GRID_W)
    col = jnp.tile(jnp.arange(GRID_W, dtype=F32), rows)
    inv = ROPE_THETA ** (-jnp.arange(0, HALF_ROT, 2, dtype=F32) / HALF_ROT)
    ar, ac = row[:, None] * inv, col[:, None] * inv
    cr, sr, cc, sc = jnp.cos(ar), jnp.sin(ar), jnp.cos(ac), jnp.sin(ac)
    cos_t = jnp.concatenate([cr, cr, cc, cc], axis=-1)
    sin_t = jnp.concatenate([-sr, sr, -sc, sc], axis=-1)
    return cos_t, sin_t


def _in_projection(x2, norm1, w_in_bf, gains, cos_t, sin_t, n_seq, tm):
    n_tok, d = x2.shape
    seq_blocks = n_seq // tm
    return pl.pallas_call(
        _inproj_kernel,
        grid=(n_tok // tm, _N_TILES),
        in_specs=[
            pl.BlockSpec((tm, d), lambda i, j: (i, 0)),
            pl.BlockSpec((1, d), lambda i, j: (0, 0)),
            pl.BlockSpec((d, IN_TN), lambda i, j: (0, j)),
            pl.BlockSpec((1, IN_TN), lambda i, j: (0, jnp.maximum(j - 1, 0))),
            pl.BlockSpec((tm, HEAD_DIM), lambda i, j: (i % seq_blocks, 0)),
            pl.BlockSpec((tm, HEAD_DIM), lambda i, j: (i % seq_blocks, 0)),
        ],
        out_specs=[pl.BlockSpec((tm, IN_TN), lambda i, j: (i, jnp.maximum(j - 1, 0))),
                   pl.BlockSpec((tm, IN_TN), lambda i, j: (i, 0))],
        out_shape=[jax.ShapeDtypeStruct((n_tok, _MAIN_TILES * IN_TN), BF16),
                   jax.ShapeDtypeStruct((n_tok, IN_TN), BF16)],
        scratch_shapes=[pltpu.VMEM((tm, d), BF16), pltpu.VMEM((tm, IN_TN), F32)],
        compiler_params=_params(("parallel", "arbitrary"), 48),
        name="in_projection",
    )(x2, norm1.reshape(1, d), w_in_bf, gains, cos_t, sin_t)


ATT_TQ = 128
ATT_TKC = 256
ATT_UNROLL = 32
LOG2E = 1.4426950408889634


def _flash_kernel(q_ref, k_ref, v_ref, o_ref, vt_ref, *, n_chunks, unroll):
    @pl.when(pl.program_id(2) == 0)
    def _():
        def transpose_chunk(c, carry):
            st = pl.multiple_of(c * ATT_TKC, ATT_TKC)
            vt_ref[:, pl.ds(st, ATT_TKC)] = v_ref[pl.ds(st, ATT_TKC), :].T
            return carry
        lax.fori_loop(0, n_chunks, transpose_chunk, 0)

    q = q_ref[...]
    tq = q.shape[0]
    q_t = jnp.concatenate([q[:, g * HEAD_DIM:(g + 1) * HEAD_DIM].T for g in range(GROUP)], axis=1)
    nq = GROUP * tq

    def body(c, carry):
        m, l, acc = carry
        starts = [pl.multiple_of((c * unroll + u) * ATT_TKC, ATT_TKC) for u in range(unroll)]
        scores = [jnp.dot(k_ref[pl.ds(st, ATT_TKC), :], q_t, preferred_element_type=F32)
                  for st in starts]
        for st, s in zip(starts, scores):
            m_new = jnp.maximum(m, jnp.max(s, axis=0, keepdims=True))
            alpha = jnp.exp2(m - m_new)
            p = jnp.exp2(s - m_new)
            l = alpha * l + jnp.sum(p, axis=0, keepdims=True)
            pv = jnp.dot(vt_ref[:, pl.ds(st, ATT_TKC)], p.astype(BF16), preferred_element_type=F32)
            acc = alpha * acc + pv
            m = m_new
        return m, l, acc

    init = (jnp.full((1, nq), -jnp.inf, F32), jnp.zeros((1, nq), F32), jnp.zeros((HEAD_DIM, nq), F32))
    _, l, acc = lax.fori_loop(0, n_chunks // unroll, body, init)
    o_t = acc / l
    for g in range(GROUP):
        o_ref[:, g * HEAD_DIM:(g + 1) * HEAD_DIM] = o_t[:, g * tq:(g + 1) * tq].T


def _attention_a(proj, b, n, tq):
    n_tok = b * n
    qb = n // tq
    k_col = W_A // HEAD_DIM
    v_col = (W_A + KVW) // HEAD_DIM
    n_chunks = n // ATT_TKC
    unroll = math.gcd(n_chunks, ATT_UNROLL)
    return pl.pallas_call(
        functools.partial(_flash_kernel, n_chunks=n_chunks, unroll=unroll),
        grid=(b, KV_A, qb),
        in_specs=[
            pl.BlockSpec((tq, GROUP * HEAD_DIM), lambda bi, h, qi: (bi * qb + qi, h)),
            pl.BlockSpec((n, HEAD_DIM), lambda bi, h, qi: (bi, k_col + h)),
            pl.BlockSpec((n, HEAD_DIM), lambda bi, h, qi: (bi, v_col + h)),
        ],
        out_specs=pl.BlockSpec((tq, GROUP * HEAD_DIM), lambda bi, h, qi: (bi * qb + qi, h)),
        out_shape=jax.ShapeDtypeStruct((n_tok, W_A), F32),
        scratch_shapes=[pltpu.VMEM((HEAD_DIM, n), BF16)],
        compiler_params=_params(("arbitrary", "arbitrary", "arbitrary"), 48),
        name="attention_global",
    )(proj, proj, proj)


WIN_TQ = 2 * WINDOW
WIN_TK = WIN_TQ + 2 * WINDOW


def _t5_buckets(rel):
    nb = REL_BUCKETS // 2
    max_exact = nb // 2
    ret = jnp.where(rel > 0, nb, 0)
    n = jnp.abs(rel)
    nf = jnp.maximum(n, 1).astype(F32)
    large = max_exact + (jnp.log(nf / max_exact) / math.log(REL_MAX_DIST / max_exact)
                         * (nb - max_exact)).astype(I32)
    large = jnp.minimum(large, nb - 1)
    return ret + jnp.where(n < max_exact, n, large)


def _window_bias(rel_bias):
    span = WIN_TQ + WIN_TK
    kk = jnp.arange(span)
    rel = jnp.where(kk < WIN_TK, kk, kk - span) - WINDOW
    tbl = jnp.where((jnp.abs(rel) <= WINDOW)[:, None], rel_bias[_t5_buckets(rel)].astype(F32), NEG_INF)
    flat = jnp.tile(tbl.T, (1, WIN_TQ))[:, :WIN_TQ * (span - 1)]
    bias = flat.reshape(H_B, WIN_TQ, span - 1)[:, :, :WIN_TK]
    j = jnp.arange(WIN_TK)
    before, after = j < WINDOW, j >= WINDOW + WIN_TQ
    masks = jnp.stack([jnp.zeros_like(before), before, after, jnp.logical_or(before, after)])
    bias = jnp.where(masks[:, None, None, :], NEG_INF, bias[None])
    bias = bias.reshape(4, KV_B, GROUP, WIN_TQ, WIN_TK).transpose(0, 1, 4, 2, 3)
    return bias.reshape(4, KV_B, WIN_TK, GROUP * WIN_TQ) * LOG2E


def _window_kernel(sink_ref, q_ref, kp_ref, kc_ref, kn_ref, vp_ref, vc_ref, vn_ref, bias_ref, o_ref):
    h = pl.program_id(0)
    q = q_ref[...]
    q_all = jnp.concatenate([q[:, g * HEAD_DIM:(g + 1) * HEAD_DIM] for g in range(GROUP)], axis=0)
    kcat = jnp.concatenate([kp_ref[...], kc_ref[...], kn_ref[...]], axis=0)
    vcat = jnp.concatenate([vp_ref[...], vc_ref[...], vn_ref[...]], axis=0)
    s = lax.dot_general(kcat, q_all, (((1,), (1,)), ((), ())), preferred_element_type=F32) + bias_ref[0, 0]
    head = lax.broadcasted_iota(I32, (1, GROUP * WIN_TQ), 1) // WIN_TQ
    snk = jnp.zeros((1, GROUP * WIN_TQ), F32)
    for g in range(GROUP):
        snk = jnp.where(head == g, sink_ref[h * GROUP + g] * LOG2E, snk)
    m = jnp.maximum(jnp.max(s, axis=0, keepdims=True), snk)
    e = jnp.exp2(s - m)
    denom = jnp.sum(e, axis=0, keepdims=True) + jnp.exp2(snk - m)
    o_t = lax.dot_general(vcat, e.astype(BF16), (((0,), (0,)), ((), ())), preferred_element_type=F32) / denom
    for g in range(GROUP):
        o_ref[:, g * HEAD_DIM:(g + 1) * HEAD_DIM] = o_t[:, g * WIN_TQ:(g + 1) * WIN_TQ].T


def _attention_b(proj, v_b, sink, bias, b, n):
    n_tok = b * n
    qb = n // WIN_TQ
    nb128 = n // WINDOW
    base = W_A + 2 * KVW
    q_col = base // (GROUP * HEAD_DIM)
    k_col = (base + W_B) // HEAD_DIM
    v_col = 0

    def prev_map(col):
        return lambda h, bi, i, s: (bi * nb128 + jnp.maximum(2 * i - 1, 0), col + h)

    def cur_map(col):
        return lambda h, bi, i, s: (bi * qb + i, col + h)

    def next_map(col):
        return lambda h, bi, i, s: (bi * nb128 + jnp.minimum(2 * i + 2, nb128 - 1), col + h)

    grid_spec = pltpu.PrefetchScalarGridSpec(
        num_scalar_prefetch=1,
        grid=(KV_B, b, qb),
        in_specs=[
            pl.BlockSpec((WIN_TQ, GROUP * HEAD_DIM), lambda h, bi, i, s: (bi * qb + i, q_col + h)),
            pl.BlockSpec((WINDOW, HEAD_DIM), prev_map(k_col)),
            pl.BlockSpec((WIN_TQ, HEAD_DIM), cur_map(k_col)),
            pl.BlockSpec((WINDOW, HEAD_DIM), next_map(k_col)),
            pl.BlockSpec((WINDOW, HEAD_DIM), prev_map(v_col)),
            pl.BlockSpec((WIN_TQ, HEAD_DIM), cur_map(v_col)),
            pl.BlockSpec((WINDOW, HEAD_DIM), next_map(v_col)),
            pl.BlockSpec((1, 1, WIN_TK, GROUP * WIN_TQ),
                         lambda h, bi, i, s: (jnp.where(i == 0, 1, 0) + jnp.where(i == qb - 1, 2, 0), h, 0, 0)),
        ],
        out_specs=pl.BlockSpec((WIN_TQ, GROUP * HEAD_DIM), lambda h, bi, i, s: (bi * qb + i, h)),
    )
    return pl.pallas_call(
        _window_kernel,
        grid_spec=grid_spec,
        out_shape=jax.ShapeDtypeStruct((n_tok, W_B), F32),
        compiler_params=_params(("parallel", "parallel", "parallel"), 32),
        name="attention_window",
    )(sink, proj, proj, proj, proj, v_b, v_b, v_b, bias)


def _outproj_kernel(oa_ref, ob_ref, ga_ref, gb_ref, w_ref, x_ref, o_ref, mix_ref):
    j = pl.program_id(1)

    @pl.when(j == 0)
    def _():
        for src, gain, lo in ((oa_ref, ga_ref, 0), (ob_ref, gb_ref, W_A)):
            o = src[...]
            ms = jnp.mean(o * o, axis=-1, keepdims=True)
            mix_ref[:, lo:lo + o.shape[1]] = (o * lax.rsqrt(ms + NORM_EPS) * gain[...]).astype(BF16)

    o_ref[...] = x_ref[...] + jnp.dot(mix_ref[...], w_ref[...], preferred_element_type=F32)


def _out_projection(oa, ob, gain_a, gain_b, w_out_bf, x2, tm, tn):
    n_tok, d = x2.shape
    return pl.pallas_call(
        _outproj_kernel,
        grid=(n_tok // tm, d // tn),
        in_specs=[
            pl.BlockSpec((tm, W_A), lambda i, j: (i, 0)),
            pl.BlockSpec((tm, W_B), lambda i, j: (i, 0)),
            pl.BlockSpec((1, W_A), lambda i, j: (0, 0)),
            pl.BlockSpec((1, W_B), lambda i, j: (0, 0)),
            pl.BlockSpec((W_A + W_B, tn), lambda i, j: (0, j)),
            pl.BlockSpec((tm, tn), lambda i, j: (i, j)),
        ],
        out_specs=pl.BlockSpec((tm, tn), lambda i, j: (i, j)),
        out_shape=jax.ShapeDtypeStruct((n_tok, d), F32),
        scratch_shapes=[pltpu.VMEM((tm, W_A + W_B), BF16)],
        compiler_params=_params(("parallel", "arbitrary"), 48),
        name="out_projection",
    )(oa, ob, gain_a.reshape(1, W_A), gain_b.reshape(1, W_B), w_out_bf, x2)


def _router_kernel(x_ref, g_ref, wr_ref, hp_ref, aff_ref):
    x = x_ref[...]
    ms = jnp.mean(x * x, axis=-1, keepdims=True)
    hb = (x * lax.rsqrt(ms + NORM_EPS) * g_ref[...]).astype(BF16)
    logits = jnp.dot(hb, wr_ref[...], preferred_element_type=F32)
    e = jnp.exp(logits - jnp.max(logits, axis=-1, keepdims=True))
    aff_ref[...] = e / jnp.sum(e, axis=-1, keepdims=True)
    half = hb.shape[1] // 2
    bits = pltpu.bitcast(hb.astype(F32), U32)
    hp_ref[...] = (bits[:, :half] >> 16) | (bits[:, half:] & jnp.uint32(0xFFFF0000))


def _router(x1, norm2, w_router_bf, tm):
    n_tok, d = x1.shape
    return pl.pallas_call(
        _router_kernel,
        grid=(n_tok // tm,),
        in_specs=[
            pl.BlockSpec((tm, d), lambda i: (i, 0)),
            pl.BlockSpec((1, d), lambda i: (0, 0)),
            pl.BlockSpec((d, N_EXPERTS), lambda i: (0, 0)),
        ],
        out_specs=[pl.BlockSpec((tm, d // 2), lambda i: (i, 0)),
                   pl.BlockSpec((tm, N_EXPERTS), lambda i: (i, 0))],
        out_shape=[jax.ShapeDtypeStruct((n_tok, d // 2), U32),
                   jax.ShapeDtypeStruct((n_tok, N_EXPERTS), F32)],
        compiler_params=_params(("parallel",), 48),
        name="router",
    )(x1, norm2.reshape(1, d), w_router_bf)


SEL_T = 128
SEL_CHUNK = 1024
RANK_BITS = 4
assert N_EXPERTS <= 1 << RANK_BITS


def _slots_kernel(aff_ref, slot_ref, tokrank_ref, tokcnt_ref, off_ref, thr_ref, need_ref, carry_ref, *, cap, n_tok):
    t = pl.program_id(0)
    n_chunks = n_tok // SEL_CHUNK

    def count(pred_fn):
        def body(c, tot):
            bits = pltpu.bitcast(aff_ref[pl.ds(c * SEL_CHUNK, SEL_CHUNK), :], I32)
            return tot + jnp.sum(pred_fn(bits).astype(I32), axis=0, keepdims=True)
        return lax.fori_loop(0, n_chunks, body, jnp.zeros((1, N_EXPERTS), I32))

    @pl.when(t == 0)
    def _():
        def bit_body(b, thr):
            cand = thr | jnp.left_shift(jnp.int32(1), 30 - b)
            return jnp.where(count(lambda bits: bits >= cand) >= cap, cand, thr)
        thr = lax.fori_loop(0, 31, bit_body, jnp.zeros((1, N_EXPERTS), I32))
        thr_ref[...] = thr
        need_ref[...] = (cap - count(lambda bits: bits > thr)).astype(F32)
        carry_ref[...] = jnp.zeros(carry_ref.shape, F32)

    a = aff_ref[pl.ds(t * SEL_T, SEL_T), :]
    bits = pltpu.bitcast(a, I32)
    thr = thr_ref[...]
    gt = bits > thr
    eq = bits == thr
    r = lax.broadcasted_iota(I32, (SEL_T, SEL_T), 0)
    c = lax.broadcasted_iota(I32, (SEL_T, SEL_T), 1)
    tri = jnp.where(r >= c, 1.0, 0.0).astype(BF16)
    eq_f = jnp.where(eq, 1.0, 0.0)
    cum_eq = jnp.dot(tri, eq_f.astype(BF16), preferred_element_type=F32) + carry_ref[0:1, :]
    sel = jnp.logical_or(gt, jnp.logical_and(eq, cum_eq <= need_ref[...]))
    sel_f = jnp.where(sel, 1.0, 0.0)
    pos = jnp.dot(tri, sel_f.astype(BF16), preferred_element_type=F32) + carry_ref[1:2, :]
    off_ref[0] = carry_ref[1:2, :].astype(I32)
    carry_ref[0:1, :] = carry_ref[0:1, :] + jnp.sum(eq_f, axis=0, keepdims=True)
    carry_ref[1:2, :] = carry_ref[1:2, :] + jnp.sum(sel_f, axis=0, keepdims=True)
    slot_ref[...] = jnp.where(sel, pos - 1.0, -1.0).astype(I32)
    er = lax.broadcasted_iota(I32, (N_EXPERTS, N_EXPERTS), 0)
    ec = lax.broadcasted_iota(I32, (N_EXPERTS, N_EXPERTS), 1)
    before = jnp.where(er < ec, 1.0, 0.0).astype(BF16)
    rank = jnp.dot(sel_f.astype(BF16), before, preferred_element_type=F32).astype(I32)
    tok = t * SEL_T + lax.broadcasted_iota(I32, (SEL_T, N_EXPERTS), 0)
    tokrank_ref[...] = tok * (1 << RANK_BITS) + rank
    tokcnt_ref[...] = jnp.broadcast_to(jnp.sum(sel_f, axis=1, keepdims=True).astype(I32), (SEL_T, N_EXPERTS))


def _lists_kernel(off_ref, slot_ref, tokrank_ref, aff_ref, idx_ref, gate_ref):
    t = pl.program_id(0)

    @pl.when(t == 0)
    def _():
        idx_ref[...] = jnp.zeros(idx_ref.shape, I32)
        gate_ref[...] = jnp.zeros(gate_ref.shape, F32)

    slot = slot_ref[...]
    tokrank = tokrank_ref[...]
    a = aff_ref[...]
    for e in range(N_EXPERTS):
        base = pl.multiple_of((off_ref[t * N_EXPERTS + e] // LANE) * LANE, LANE)
        lane = base + lax.broadcasted_iota(I32, (1, 2 * LANE), 1)
        match = slot[:, e:e + 1] == lane
        win = (slice(e, e + 1), pl.ds(base, 2 * LANE))
        idx_ref[win] += jnp.sum(jnp.where(match, tokrank[:, e:e + 1], 0), axis=0, keepdims=True)
        gate_ref[win] += jnp.sum(jnp.where(match, a[:, e:e + 1], 0.0), axis=0, keepdims=True)


def _select(aff, cap):
    n_tok = aff.shape[0]
    nt = n_tok // SEL_T
    tile = pl.BlockSpec((SEL_T, N_EXPERTS), lambda t: (t, 0))
    tile_shape = jax.ShapeDtypeStruct((n_tok, N_EXPERTS), I32)
    slot, tokrank, tokcnt, offs = pl.pallas_call(
        functools.partial(_slots_kernel, cap=cap, n_tok=n_tok),
        grid=(nt,),
        in_specs=[pl.BlockSpec((n_tok, N_EXPERTS), lambda t: (0, 0))],
        out_specs=[tile, tile, tile, pl.BlockSpec((1, 1, N_EXPERTS), lambda t: (t, 0, 0))],
        out_shape=[tile_shape, tile_shape, tile_shape, jax.ShapeDtypeStruct((nt, 1, N_EXPERTS), I32)],
        scratch_shapes=[pltpu.VMEM((1, N_EXPERTS), I32), pltpu.VMEM((1, N_EXPERTS), F32),
                        pltpu.VMEM((2, N_EXPERTS), F32)],
        compiler_params=_params(("arbitrary",), 32),
        name="expert_slots",
    )(aff)
    offs = offs.reshape(nt, N_EXPERTS)
    width = cap + 2 * LANE
    tile_p = pl.BlockSpec((SEL_T, N_EXPERTS), lambda t, o: (t, 0))
    idx, gates = pl.pallas_call(
        _lists_kernel,
        grid_spec=pltpu.PrefetchScalarGridSpec(
            num_scalar_prefetch=1, grid=(nt,),
            in_specs=[tile_p, tile_p, tile_p],
            out_specs=[pl.BlockSpec((N_EXPERTS, width), lambda t, o: (0, 0)),
                       pl.BlockSpec((N_EXPERTS, width), lambda t, o: (0, 0))]),
        out_shape=[jax.ShapeDtypeStruct((N_EXPERTS, width), I32),
                   jax.ShapeDtypeStruct((N_EXPERTS, width), F32)],
        compiler_params=_params(("arbitrary",), 32),
        name="expert_lists",
    )(offs.reshape(-1), slot, tokrank, aff)
    return idx[:, :cap], gates[:, :cap], offs, tokcnt


def _gather_kernel(idx_ref, hp_ref, xe_ref, sem, *, rows):
    def row_copy(s):
        tok = idx_ref[0, 0, s] >> RANK_BITS
        return pltpu.make_async_copy(hp_ref.at[pl.ds(tok, 1)], xe_ref.at[0, pl.ds(s, 1)], sem)

    def start(s, carry):
        row_copy(s).start()
        return carry

    def wait(s, carry):
        row_copy(s).wait()
        return carry

    lax.fori_loop(0, rows, start, 0, unroll=GATHER_UNROLL)
    lax.fori_loop(0, rows, wait, 0, unroll=GATHER_UNROLL)


GATHER_UNROLL = 8


def _gather(idx, hp, cap, rows):
    half = hp.shape[1]
    n_blocks = cap // rows
    idx3 = idx.reshape(N_EXPERTS * n_blocks, 1, rows)
    return pl.pallas_call(
        functools.partial(_gather_kernel, rows=rows),
        grid=(N_EXPERTS, n_blocks),
        in_specs=[pl.BlockSpec((1, 1, rows), lambda e, r: (e * n_blocks + r, 0, 0), memory_space=pltpu.SMEM),
                  pl.BlockSpec(memory_space=pl.ANY)],
        out_specs=pl.BlockSpec((1, rows, half), lambda e, r: (e, r, 0)),
        out_shape=jax.ShapeDtypeStruct((N_EXPERTS, cap, half), U32),
        scratch_shapes=[pltpu.SemaphoreType.DMA(())],
        compiler_params=_params(("arbitrary", "arbitrary"), 32),
        name="expert_gather",
    )(idx3, hp)


def _unpack(xp):
    lo = pltpu.bitcast(xp << 16, F32).astype(BF16)
    hi = pltpu.bitcast(xp & jnp.uint32(0xFFFF0000), F32).astype(BF16)
    return lo, hi


def _ffn_up_kernel(xe_hbm, wg_ref, wu_ref, h_ref, xe_vmem, wg_bf, wu_bf, sem, *, tr):
    e, j, r = pl.program_id(0), pl.program_id(1), pl.program_id(2)

    @pl.when(jnp.logical_and(j == 0, r == 0))
    def _():
        copy = pltpu.make_async_copy(xe_hbm.at[e], xe_vmem, sem)
        copy.start()
        copy.wait()

    @pl.when(r == 0)
    def _():
        wg_bf[...] = wg_ref[0].astype(BF16)
        wu_bf[...] = wu_ref[0].astype(BF16)

    lo, hi = _unpack(xe_vmem[pl.ds(pl.multiple_of(r * tr, tr), tr), :])
    half = lo.shape[1]

    def proj(w_bf):
        return (jnp.dot(lo, w_bf[:half, :], preferred_element_type=F32)
                + jnp.dot(hi, w_bf[half:, :], preferred_element_type=F32))

    gate = proj(wg_bf)
    up = proj(wu_bf)
    h_ref[0] = (gate * jax.nn.sigmoid(gate) * up).astype(BF16)


def _ffn_up(xe, w_gate, w_up, tr, tf):
    _, cap, half = xe.shape
    d, f = w_gate.shape[1], w_gate.shape[2]
    return pl.pallas_call(
        functools.partial(_ffn_up_kernel, tr=tr),
        grid=(N_EXPERTS, f // tf, cap // tr),
        in_specs=[pl.BlockSpec(memory_space=pl.ANY),
                  pl.BlockSpec((1, d, tf), lambda e, j, r: (e, 0, j)),
                  pl.BlockSpec((1, d, tf), lambda e, j, r: (e, 0, j))],
        out_specs=pl.BlockSpec((1, tr, tf), lambda e, j, r: (e, r, j)),
        out_shape=jax.ShapeDtypeStruct((N_EXPERTS, cap, f), BF16),
        scratch_shapes=[pltpu.VMEM((cap, half), U32), pltpu.VMEM((d, tf), BF16), pltpu.VMEM((d, tf), BF16),
                        pltpu.SemaphoreType.DMA(())],
        compiler_params=_params(("arbitrary", "arbitrary", "arbitrary"), 56),
        name="expert_ffn_up",
    )(xe, w_gate, w_up)


def _pack_halves(x):
    w = x.shape[1] // 2
    bits = pltpu.bitcast(x.astype(BF16).astype(F32), U32)
    return (bits[:, :w] >> 16) | (bits[:, w:] & jnp.uint32(0xFFFF0000))


def _ffn_down_kernel(h_ref, wd_ref, g_ref, o_ref, wd_bf, *, tr):
    @pl.when(pl.program_id(2) == 0)
    def _():
        wd_bf[...] = wd_ref[0].astype(BF16)

    rows = pl.ds(pl.multiple_of(pl.program_id(2) * tr, tr), tr)
    out = jnp.dot(h_ref[0, rows, :], wd_bf[...], preferred_element_type=F32)
    o_ref[0] = _pack_halves(g_ref[0] * out)


def _ffn_down(hid, w_down, gates, tr, tn):
    _, cap, f = hid.shape
    d = w_down.shape[2]
    return pl.pallas_call(
        functools.partial(_ffn_down_kernel, tr=tr),
        grid=(N_EXPERTS, d // tn, cap // tr),
        in_specs=[pl.BlockSpec((1, cap, f), lambda e, j, r: (e, 0, 0)),
                  pl.BlockSpec((1, f, tn), lambda e, j, r: (e, 0, j)),
                  pl.BlockSpec((1, tr, 1), lambda e, j, r: (e, r, 0))],
        out_specs=pl.BlockSpec((1, tr, tn // 2), lambda e, j, r: (e, r, j)),
        out_shape=jax.ShapeDtypeStruct((N_EXPERTS, cap, d // 2), U32),
        scratch_shapes=[pltpu.VMEM((f, tn), BF16)],
        compiler_params=_params(("parallel", "parallel", "arbitrary"), 48),
        name="expert_ffn_down",
    )(hid, w_down, gates.reshape(N_EXPERTS, cap, 1))


def _combine_kernel(lo_ref, hi_ref, maxc_ref, idx_ref, x_ref, cnt_ref, con_ref, y_ref, stage_ref, sem,
                    *, nt, cap, tn):
    t = pl.program_id(0)

    def for_rows(tile, fn):
        slot = tile % 2

        def row_copy(row):
            v = idx_ref[row]
            tok = (v >> RANK_BITS) - tile * SEL_T
            return pltpu.make_async_copy(con_ref.at[pl.ds(row, 1)],
                                         stage_ref.at[slot, v & ((1 << RANK_BITS) - 1), pl.ds(tok, 1)],
                                         sem.at[slot])

        def per_row(row, carry):
            fn(row_copy(row))
            return carry

        def per_expert(e, carry):
            k = e * nt + tile
            return lax.fori_loop(e * cap + lo_ref[k], e * cap + hi_ref[k], per_row, carry)

        lax.fori_loop(0, N_EXPERTS, per_expert, 0)

    @pl.when(t == 0)
    def _():
        stage_ref[...] = jnp.zeros(stage_ref.shape, U32)
        for_rows(t, lambda copy: copy.start())

    @pl.when(t + 1 < nt)
    def _():
        for_rows(t + 1, lambda copy: copy.start())

    y_ref[...] = x_ref[...]
    for_rows(t, lambda copy: copy.wait())

    half = tn // 2
    cnt = jnp.broadcast_to(cnt_ref[:, 0:1], (SEL_T, half))
    slot = t % 2

    def add_rank(k, carry):
        live = k < cnt
        for j in range(y_ref.shape[1] // tn):
            xp = jnp.where(live, stage_ref[slot, k, :, j * half:(j + 1) * half], jnp.uint32(0))
            y_ref[:, j * tn:j * tn + half] += pltpu.bitcast(xp << 16, F32)
            y_ref[:, j * tn + half:(j + 1) * tn] += pltpu.bitcast(xp & jnp.uint32(0xFFFF0000), F32)
        return carry

    lax.fori_loop(0, maxc_ref[t], add_rank, 0)


def _combine(x1, idx, contrib, offs, tokcnt, cap, tn):
    n_tok, d = x1.shape
    nt = n_tok // SEL_T
    lo = jnp.transpose(offs)
    hi = jnp.concatenate([lo[:, 1:], jnp.full((N_EXPERTS, 1), cap, I32)], axis=1)
    maxc = jnp.max(tokcnt[:, 0].reshape(nt, SEL_T), axis=1)
    grid_spec = pltpu.PrefetchScalarGridSpec(
        num_scalar_prefetch=4,
        grid=(nt,),
        in_specs=[
            pl.BlockSpec((SEL_T, d), lambda t, *_: (t, 0)),
            pl.BlockSpec((SEL_T, N_EXPERTS), lambda t, *_: (t, 0)),
            pl.BlockSpec(memory_space=pl.ANY),
        ],
        out_specs=pl.BlockSpec((SEL_T, d), lambda t, *_: (t, 0)),
        scratch_shapes=[pltpu.VMEM((2, N_EXPERTS, SEL_T, d // 2), U32), pltpu.SemaphoreType.DMA((2,))],
    )
    return pl.pallas_call(
        functools.partial(_combine_kernel, nt=nt, cap=cap, tn=tn),
        grid_spec=grid_spec,
        out_shape=jax.ShapeDtypeStruct((n_tok, d), F32),
        compiler_params=_params(("arbitrary",), 48),
        name="expert_combine",
    )(lo.reshape(-1), hi.reshape(-1), maxc, idx.reshape(-1), x1, tokcnt,
      contrib.reshape(N_EXPERTS * cap, d // 2))


DOWN_TN = 1024
UP_TF = 256
OUT_TN = 1024


def _tiles(n_tok, n_seq, cap):
    return dict(
        tm=min(512, n_seq),
        rows=min(512, cap),
        tr=min(1024, cap),
    )


def _encoder_layer(x, p):
    b, n, d = x.shape
    n_tok = b * n
    cap = EC_FACTOR * n_tok // N_EXPERTS
    tl = _tiles(n_tok, n, cap)
    x2 = x.reshape(n_tok, d)

    proj, v_b = _in_projection(x2, p["norm1"], p["w_in"], p["gains"], p["cos"][n], p["sin"][n], n, tl["tm"])
    oa = _attention_a(proj, b, n, ATT_TQ)
    ob = _attention_b(proj, v_b, p["sink"], p["bias"], b, n)
    x1 = _out_projection(oa, ob, p["out_norm_a"], p["out_norm_b"], p["w_out"], x2, tl["tm"], OUT_TN)

    hp, aff = _router(x1, p["norm2"], p["w_router"], tl["tm"])
    idx, gates, offs, tokcnt = _select(aff, cap)
    xe = _gather(idx, hp, cap, tl["rows"])
    hid = _ffn_up(xe, p["w_gate"], p["w_up"], tl["tr"], UP_TF)
    contrib = _ffn_down(hid, p["w_down"], gates, tl["tr"], DOWN_TN)
    y = _combine(x1, idx, contrib, offs, tokcnt, cap, DOWN_TN)
    return y.reshape(b, n, d)


def kernel(x_prompt, x_sample, norm1, w_in, q_norm_a, k_norm_a, q_norm_b, k_norm_b, sink_b, rel_bias,
           out_norm_a, out_norm_b, w_out, norm2, w_router, w_gate, w_up, w_down):
    depth = norm1.shape[0]
    seqs = sorted({x_prompt.shape[1], x_sample.shape[1]})
    tables = {n: _rope_tables(n) for n in seqs}
    bias = _window_bias(rel_bias)
    ones = jnp.ones((KVW,), F32)

    layers = []
    for l in range(depth):
        gains = jnp.concatenate([jnp.tile(q_norm_a[l], H_A), jnp.tile(k_norm_a[l], KV_A), ones,
                                 jnp.tile(q_norm_b[l], H_B), jnp.tile(k_norm_b[l], KV_B), ones])
        layers.append(dict(
            norm1=norm1[l], w_in=w_in[l].astype(BF16), gains=gains.reshape(1, IN_WIDTH),
            cos={n: tables[n][0] for n in seqs}, sin={n: tables[n][1] for n in seqs},
            sink=sink_b[l], bias=bias,
            out_norm_a=out_norm_a[l], out_norm_b=out_norm_b[l], w_out=w_out[l].astype(BF16),
            norm2=norm2[l], w_router=w_router[l].astype(BF16),
            w_gate=w_gate[l], w_up=w_up[l], w_down=w_down[l],
        ))

    def run(x):
        for p in layers:
            x = _encoder_layer(x, p)
        return x

    return (run(x_prompt), run(x_sample))
```

```python
import functools
import math

import jax
import jax.numpy as jnp
from jax import lax
from jax.experimental import pallas as pl
from jax.experimental.pallas import tpu as pltpu

F32 = jnp.float32
BF16 = jnp.bfloat16
I32 = jnp.int32
U32 = jnp.uint32

HEAD_DIM = 128
H_A = 16
KV_A = 4
H_B = 16
KV_B = 4
GROUP = 4
W_A = H_A * HEAD_DIM
W_B = H_B * HEAD_DIM
KVW = KV_A * HEAD_DIM
IN_WIDTH = W_A + 2 * KVW + W_B + 2 * KVW
GRID_W = 64
HALF_ROT = HEAD_DIM // 2
ROPE_THETA = 10000.0
REL_BUCKETS = 32
REL_MAX_DIST = 128
WINDOW = 128
N_EXPERTS = 16
EC_FACTOR = 2
NORM_EPS = 1e-6
NEG_INF = -1e30
SCALE = 1.0 / math.sqrt(HEAD_DIM)

V7X_VMEM_BYTES = 64 * 1024 * 1024
LANE = 128


def _vmem(mib):
    assert mib * 1024 * 1024 < V7X_VMEM_BYTES
    return mib * 1024 * 1024


def _params(sem, mib):
    return pltpu.CompilerParams(dimension_semantics=sem, vmem_limit_bytes=_vmem(mib))


IN_TN = 512
_ROPE_TILES = 5
_VA_TILE = 5
_N_TILES = IN_WIDTH // IN_TN
_MAIN_TILES = _N_TILES - 1


def _inproj_kernel(x_ref, g1_ref, w_ref, gain_ref, cos_ref, sin_ref, o_ref, vb_ref, h_ref, raw_ref):
    j = pl.program_id(1)

    @pl.when(j == 0)
    def _():
        x = x_ref[...]
        ms = jnp.mean(x * x, axis=-1, keepdims=True)
        h_ref[...] = (x * lax.rsqrt(ms + NORM_EPS) * g1_ref[...]).astype(BF16)
        raw_ref[...] = jnp.zeros(raw_ref.shape, F32)

    prev = raw_ref[...]
    acc = jnp.dot(h_ref[...], w_ref[...], preferred_element_type=F32)
    jp = j - 1
    is_v = jp == _VA_TILE
    is_rope = jp < _ROPE_TILES
    is_q = jnp.logical_or(jp < W_A // IN_TN,
                          jnp.logical_and(jp > _VA_TILE, jp < (IN_WIDTH - 2 * KVW) // IN_TN))
    outs = []
    for hh in range(IN_TN // HEAD_DIM):
        a = prev[:, hh * HEAD_DIM:(hh + 1) * HEAD_DIM]
        ms = jnp.mean(a * a, axis=-1, keepdims=True)
        r = jnp.where(is_v, 1.0, lax.rsqrt(ms + NORM_EPS))
        outs.append(a * r * gain_ref[:, hh * HEAD_DIM:(hh + 1) * HEAD_DIM])
    y = jnp.concatenate(outs, axis=-1)
    reps = IN_TN // HEAD_DIM
    c = jnp.concatenate([jnp.where(is_rope, cos_ref[...], 1.0)] * reps, axis=-1)
    s = jnp.concatenate([jnp.where(is_rope, sin_ref[...], 0.0)] * reps, axis=-1)
    lane = lax.broadcasted_iota(I32, y.shape, 1)
    first_half = (lane % (2 * (HALF_ROT // 2))) < (HALF_ROT // 2)
    partner = jnp.where(first_half,
                        pltpu.roll(y, IN_TN - HALF_ROT // 2, axis=1),
                        pltpu.roll(y, HALF_ROT // 2, axis=1))
    o_ref[...] = ((y * c + partner * s) * jnp.where(is_q, SCALE * LOG2E, 1.0)).astype(BF16)
    raw_ref[...] = acc

    @pl.when(j == _N_TILES - 1)
    def _():
        vb_ref[...] = acc.astype(BF16)


def _rope_tables(n):
    rows = n // GRID_W
    row = jnp.repeat(jnp.arange(rows, dtype=F32), GRID_W)
    col = jnp.tile(jnp.arange(GRID_W, dtype=F32), rows)
    inv = ROPE_THETA ** (-jnp.arange(0, HALF_ROT, 2, dtype=F32) / HALF_ROT)
    ar, ac = row[:, None] * inv, col[:, None] * inv
    cr, sr, cc, sc = jnp.cos(ar), jnp.sin(ar), jnp.cos(ac), jnp.sin(ac)
    cos_t = jnp.concatenate([cr, cr, cc, cc], axis=-1)
    sin_t = jnp.concatenate([-sr, sr, -sc, sc], axis=-1)
    return cos_t, sin_t


def _in_projection(x2, norm1, w_in_bf, gains, cos_t, sin_t, n_seq, tm):
    n_tok, d = x2.shape
    seq_blocks = n_seq // tm
    return pl.pallas_call(
        _inproj_kernel,
        grid=(n_tok // tm, _N_TILES),
        in_specs=[
            pl.BlockSpec((tm, d), lambda i, j: (i, 0)),
            pl.BlockSpec((1, d), lambda i, j: (0, 0)),
            pl.BlockSpec((d, IN_TN), lambda i, j: (0, j)),
            pl.BlockSpec((1, IN_TN), lambda i, j: (0, jnp.maximum(j - 1, 0))),
            pl.BlockSpec((tm, HEAD_DIM), lambda i, j: (i % seq_blocks, 0)),
            pl.BlockSpec((tm, HEAD_DIM), lambda i, j: (i % seq_blocks, 0)),
        ],
        out_specs=[pl.BlockSpec((tm, IN_TN), lambda i, j: (i, jnp.maximum(j - 1, 0))),
                   pl.BlockSpec((tm, IN_TN), lambda i, j: (i, 0))],
        out_shape=[jax.ShapeDtypeStruct((n_tok, _MAIN_TILES * IN_TN), BF16),
                   jax.ShapeDtypeStruct((n_tok, IN_TN), BF16)],
        scratch_shapes=[pltpu.VMEM((tm, d), BF16), pltpu.VMEM((tm, IN_TN), F32)],
        compiler_params=_params(("parallel", "arbitrary"), 48),
        name="in_projection",
    )(x2, norm1.reshape(1, d), w_in_bf, gains, cos_t, sin_t)


ATT_TQ = 128
ATT_TKC = 256
ATT_UNROLL = 32
LOG2E = 1.4426950408889634


ATT_PV_GROUP = 4
ATT_BOUND_LIMIT = 40.0
ATT_BOUND_SLACK = 1.001


def _flash_kernel(q_ref, k_ref, v_ref, o_ref, vt_ref, kmax_ref, *, n_chunks, unroll):
    @pl.when(pl.program_id(2) == 0)
    def _():
        def prepare_chunk(c, kmax2):
            st = pl.multiple_of(c * ATT_TKC, ATT_TKC)
            vt_ref[:, pl.ds(st, ATT_TKC)] = v_ref[pl.ds(st, ATT_TKC), :].T
            kf = k_ref[pl.ds(st, ATT_TKC), :].astype(F32)
            return jnp.maximum(kmax2, jnp.max(jnp.sum(kf * kf, axis=1, keepdims=True), axis=0, keepdims=True))
        kmax2 = lax.fori_loop(0, n_chunks, prepare_chunk, jnp.zeros((1, 1), F32))
        kmax_ref[...] = jnp.broadcast_to(jnp.sqrt(kmax2), kmax_ref.shape)

    q = q_ref[...]
    tq = q.shape[0]
    q_t = jnp.concatenate([q[:, g * HEAD_DIM:(g + 1) * HEAD_DIM].T for g in range(GROUP)], axis=1)
    nq = GROUP * tq
    qf = q_t.astype(F32)
    bound = jnp.sqrt(jnp.sum(qf * qf, axis=0, keepdims=True)) * kmax_ref[...] * ATT_BOUND_SLACK

    def write(acc, l):
        o_t = acc / l
        for g in range(GROUP):
            o_ref[:, g * HEAD_DIM:(g + 1) * HEAD_DIM] = o_t[:, g * tq:(g + 1) * tq].T

    def start(c, u):
        return pl.multiple_of((c * unroll + u) * ATT_TKC, ATT_TKC)

    bounded = jnp.max(bound) <= ATT_BOUND_LIMIT

    @pl.when(bounded)
    def _():
        group = math.gcd(unroll, ATT_PV_GROUP)

        def body(c, carry):
            l, acc = carry
            for g0 in range(0, unroll, group):
                probs = []
                for u in range(g0, g0 + group):
                    s = jnp.dot(k_ref[pl.ds(start(c, u), ATT_TKC), :], q_t, preferred_element_type=F32)
                    p = jnp.exp2(s - bound)
                    l = l + jnp.sum(p, axis=0, keepdims=True)
                    probs.append(p.astype(BF16))
                keys = pl.ds(pl.multiple_of((c * unroll + g0) * ATT_TKC, group * ATT_TKC), group * ATT_TKC)
                acc = acc + jnp.dot(vt_ref[:, keys], jnp.concatenate(probs, axis=0), preferred_element_type=F32)
            return l, acc

        init = (jnp.zeros((1, nq), F32), jnp.zeros((HEAD_DIM, nq), F32))
        l, acc = lax.fori_loop(0, n_chunks // unroll, body, init)
        write(acc, l)

    @pl.when(jnp.logical_not(bounded))
    def _():
        def body(c, carry):
            m, l, acc = carry
            scores = [jnp.dot(k_ref[pl.ds(start(c, u), ATT_TKC), :], q_t, preferred_element_type=F32)
                      for u in range(unroll)]
            for u, s in enumerate(scores):
                m_new = jnp.maximum(m, jnp.max(s, axis=0, keepdims=True))
                alpha = jnp.exp2(m - m_new)
                p = jnp.exp2(s - m_new)
                l = alpha * l + jnp.sum(p, axis=0, keepdims=True)
                pv = jnp.dot(vt_ref[:, pl.ds(start(c, u), ATT_TKC)], p.astype(BF16), preferred_element_type=F32)
                acc = alpha * acc + pv
                m = m_new
            return m, l, acc

        init = (jnp.full((1, nq), -jnp.inf, F32), jnp.zeros((1, nq), F32), jnp.zeros((HEAD_DIM, nq), F32))
        _, l, acc = lax.fori_loop(0, n_chunks // unroll, body, init)
        write(acc, l)


def _attention_a(proj, b, n, tq):
    n_tok = b * n
    qb = n // tq
    k_col = W_A // HEAD_DIM
    v_col = (W_A + KVW) // HEAD_DIM
    n_chunks = n // ATT_TKC
    unroll = math.gcd(n_chunks, ATT_UNROLL)
    return pl.pallas_call(
        functools.partial(_flash_kernel, n_chunks=n_chunks, unroll=unroll),
        grid=(b, KV_A, qb),
        in_specs=[
            pl.BlockSpec((tq, GROUP * HEAD_DIM), lambda bi, h, qi: (bi * qb + qi, h)),
            pl.BlockSpec((n, HEAD_DIM), lambda bi, h, qi: (bi, k_col + h)),
            pl.BlockSpec((n, HEAD_DIM), lambda bi, h, qi: (bi, v_col + h)),
        ],
        out_specs=pl.BlockSpec((tq, GROUP * HEAD_DIM), lambda bi, h, qi: (bi * qb + qi, h)),
        out_shape=jax.ShapeDtypeStruct((n_tok, W_A), F32),
        scratch_shapes=[pltpu.VMEM((HEAD_DIM, n), BF16), pltpu.VMEM((1, GROUP * tq), F32)],
        compiler_params=_params(("arbitrary", "arbitrary", "arbitrary"), 48),
        name="attention_global",
    )(proj, proj, proj)


WIN_TQ = 2 * WINDOW
WIN_TK = WIN_TQ + 2 * WINDOW


def _t5_buckets(rel):
    nb = REL_BUCKETS // 2
    max_exact = nb // 2
    ret = jnp.where(rel > 0, nb, 0)
    n = jnp.abs(rel)
    nf = jnp.maximum(n, 1).astype(F32)
    large = max_exact + (jnp.log(nf / max_exact) / math.log(REL_MAX_DIST / max_exact)
                         * (nb - max_exact)).astype(I32)
    large = jnp.minimum(large, nb - 1)
    return ret + jnp.where(n < max_exact, n, large)


def _window_bias(rel_bias):
    span = WIN_TQ + WIN_TK
    kk = jnp.arange(span)
    rel = jnp.where(kk < WIN_TK, kk, kk - span) - WINDOW
    tbl = jnp.where((jnp.abs(rel) <= WINDOW)[:, None], rel_bias[_t5_buckets(rel)].astype(F32), NEG_INF)
    flat = jnp.tile(tbl.T, (1, WIN_TQ))[:, :WIN_TQ * (span - 1)]
    bias = flat.reshape(H_B, WIN_TQ, span - 1)[:, :, :WIN_TK]
    j = jnp.arange(WIN_TK)
    before, after = j < WINDOW, j >= WINDOW + WIN_TQ
    masks = jnp.stack([jnp.zeros_like(before), before, after, jnp.logical_or(before, after)])
    bias = jnp.where(masks[:, None, None, :], NEG_INF, bias[None])
    bias = bias.reshape(4, KV_B, GROUP, WIN_TQ, WIN_TK).transpose(0, 1, 4, 2, 3)
    return bias.reshape(4, KV_B, WIN_TK, GROUP * WIN_TQ) * LOG2E


def _window_kernel(sink_ref, q_ref, kp_ref, kc_ref, kn_ref, vp_ref, vc_ref, vn_ref, bias_ref, o_ref):
    h = pl.program_id(0)
    q = q_ref[...]
    q_all = jnp.concatenate([q[:, g * HEAD_DIM:(g + 1) * HEAD_DIM] for g in range(GROUP)], axis=0)
    kcat = jnp.concatenate([kp_ref[...], kc_ref[...], kn_ref[...]], axis=0)
    vcat = jnp.concatenate([vp_ref[...], vc_ref[...], vn_ref[...]], axis=0)
    s = lax.dot_general(kcat, q_all, (((1,), (1,)), ((), ())), preferred_element_type=F32) + bias_ref[0, 0]
    head = lax.broadcasted_iota(I32, (1, GROUP * WIN_TQ), 1) // WIN_TQ
    snk = jnp.zeros((1, GROUP * WIN_TQ), F32)
    for g in range(GROUP):
        snk = jnp.where(head == g, sink_ref[h * GROUP + g] * LOG2E, snk)
    m = jnp.maximum(jnp.max(s, axis=0, keepdims=True), snk)
    e = jnp.exp2(s - m)
    denom = jnp.sum(e, axis=0, keepdims=True) + jnp.exp2(snk - m)
    o_t = lax.dot_general(vcat, e.astype(BF16), (((0,), (0,)), ((), ())), preferred_element_type=F32) / denom
    for g in range(GROUP):
        o_ref[:, g * HEAD_DIM:(g + 1) * HEAD_DIM] = o_t[:, g * WIN_TQ:(g + 1) * WIN_TQ].T


def _attention_b(proj, v_b, sink, bias, b, n):
    n_tok = b * n
    qb = n // WIN_TQ
    nb128 = n // WINDOW
    base = W_A + 2 * KVW
    q_col = base // (GROUP * HEAD_DIM)
    k_col = (base + W_B) // HEAD_DIM
    v_col = 0

    def prev_map(col):
        return lambda h, bi, i, s: (bi * nb128 + jnp.maximum(2 * i - 1, 0), col + h)

    def cur_map(col):
        return lambda h, bi, i, s: (bi * qb + i, col + h)

    def next_map(col):
        return lambda h, bi, i, s: (bi * nb128 + jnp.minimum(2 * i + 2, nb128 - 1), col + h)

    grid_spec = pltpu.PrefetchScalarGridSpec(
        num_scalar_prefetch=1,
        grid=(KV_B, b, qb),
        in_specs=[
            pl.BlockSpec((WIN_TQ, GROUP * HEAD_DIM), lambda h, bi, i, s: (bi * qb + i, q_col + h)),
            pl.BlockSpec((WINDOW, HEAD_DIM), prev_map(k_col)),
            pl.BlockSpec((WIN_TQ, HEAD_DIM), cur_map(k_col)),
            pl.BlockSpec((WINDOW, HEAD_DIM), next_map(k_col)),
            pl.BlockSpec((WINDOW, HEAD_DIM), prev_map(v_col)),
            pl.BlockSpec((WIN_TQ, HEAD_DIM), cur_map(v_col)),
            pl.BlockSpec((WINDOW, HEAD_DIM), next_map(v_col)),
            pl.BlockSpec((1, 1, WIN_TK, GROUP * WIN_TQ),
                         lambda h, bi, i, s: (jnp.where(i == 0, 1, 0) + jnp.where(i == qb - 1, 2, 0), h, 0, 0)),
        ],
        out_specs=pl.BlockSpec((WIN_TQ, GROUP * HEAD_DIM), lambda h, bi, i, s: (bi * qb + i, h)),
    )
    return pl.pallas_call(
        _window_kernel,
        grid_spec=grid_spec,
        out_shape=jax.ShapeDtypeStruct((n_tok, W_B), F32),
        compiler_params=_params(("parallel", "parallel", "parallel"), 32),
        name="attention_window",
    )(sink, proj, proj, proj, proj, v_b, v_b, v_b, bias)


def _outproj_kernel(oa_ref, ob_ref, ga_ref, gb_ref, w_ref, x_ref, o_ref, mix_ref):
    j = pl.program_id(1)

    @pl.when(j == 0)
    def _():
        for src, gain, lo in ((oa_ref, ga_ref, 0), (ob_ref, gb_ref, W_A)):
            o = src[...]
            ms = jnp.mean(o * o, axis=-1, keepdims=True)
            mix_ref[:, lo:lo + o.shape[1]] = (o * lax.rsqrt(ms + NORM_EPS) * gain[...]).astype(BF16)

    o_ref[...] = x_ref[...] + jnp.dot(mix_ref[...], w_ref[...], preferred_element_type=F32)


def _out_projection(oa, ob, gain_a, gain_b, w_out_bf, x2, tm, tn):
    n_tok, d = x2.shape
    return pl.pallas_call(
        _outproj_kernel,
        grid=(n_tok // tm, d // tn),
        in_specs=[
            pl.BlockSpec((tm, W_A), lambda i, j: (i, 0)),
            pl.BlockSpec((tm, W_B), lambda i, j: (i, 0)),
            pl.BlockSpec((1, W_A), lambda i, j: (0, 0)),
            pl.BlockSpec((1, W_B), lambda i, j: (0, 0)),
            pl.BlockSpec((W_A + W_B, tn), lambda i, j: (0, j)),
            pl.BlockSpec((tm, tn), lambda i, j: (i, j)),
        ],
        out_specs=pl.BlockSpec((tm, tn), lambda i, j: (i, j)),
        out_shape=jax.ShapeDtypeStruct((n_tok, d), F32),
        scratch_shapes=[pltpu.VMEM((tm, W_A + W_B), BF16)],
        compiler_params=_params(("parallel", "arbitrary"), 48),
        name="out_projection",
    )(oa, ob, gain_a.reshape(1, W_A), gain_b.reshape(1, W_B), w_out_bf, x2)


def _router_kernel(x_ref, g_ref, wr_ref, hp_ref, aff_ref):
    x = x_ref[...]
    ms = jnp.mean(x * x, axis=-1, keepdims=True)
    hb = (x * lax.rsqrt(ms + NORM_EPS) * g_ref[...]).astype(BF16)
    logits = jnp.dot(hb, wr_ref[...], preferred_element_type=F32)
    e = jnp.exp(logits - jnp.max(logits, axis=-1, keepdims=True))
    aff_ref[...] = e / jnp.sum(e, axis=-1, keepdims=True)
    half = hb.shape[1] // 2
    bits = pltpu.bitcast(hb.astype(F32), U32)
    hp_ref[...] = (bits[:, :half] >> 16) | (bits[:, half:] & jnp.uint32(0xFFFF0000))


def _router(x1, norm2, w_router_bf, tm):
    n_tok, d = x1.shape
    return pl.pallas_call(
        _router_kernel,
        grid=(n_tok // tm,),
        in_specs=[
            pl.BlockSpec((tm, d), lambda i: (i, 0)),
            pl.BlockSpec((1, d), lambda i: (0, 0)),
            pl.BlockSpec((d, N_EXPERTS), lambda i: (0, 0)),
        ],
        out_specs=[pl.BlockSpec((tm, d // 2), lambda i: (i, 0)),
                   pl.BlockSpec((tm, N_EXPERTS), lambda i: (i, 0))],
        out_shape=[jax.ShapeDtypeStruct((n_tok, d // 2), U32),
                   jax.ShapeDtypeStruct((n_tok, N_EXPERTS), F32)],
        compiler_params=_params(("parallel",), 48),
        name="router",
    )(x1, norm2.reshape(1, d), w_router_bf)


SEL_T = 128
SEL_CHUNK = 1024
RANK_BITS = 4
assert N_EXPERTS <= 1 << RANK_BITS


def _slots_kernel(aff_ref, slot_ref, tokrank_ref, tokcnt_ref, off_ref, thr_ref, need_ref, carry_ref, *, cap, n_tok):
    t = pl.program_id(0)
    n_chunks = n_tok // SEL_CHUNK

    def count(pred_fn):
        def body(c, tot):
            bits = pltpu.bitcast(aff_ref[pl.ds(c * SEL_CHUNK, SEL_CHUNK), :], I32)
            return tot + jnp.sum(pred_fn(bits).astype(I32), axis=0, keepdims=True)
        return lax.fori_loop(0, n_chunks, body, jnp.zeros((1, N_EXPERTS), I32))

    @pl.when(t == 0)
    def _():
        def bit_body(b, thr):
            cand = thr | jnp.left_shift(jnp.int32(1), 30 - b)
            return jnp.where(count(lambda bits: bits >= cand) >= cap, cand, thr)
        thr = lax.fori_loop(0, 31, bit_body, jnp.zeros((1, N_EXPERTS), I32))
        thr_ref[...] = thr
        need_ref[...] = (cap - count(lambda bits: bits > thr)).astype(F32)
        carry_ref[...] = jnp.zeros(carry_ref.shape, F32)

    a = aff_ref[pl.ds(t * SEL_T, SEL_T), :]
    bits = pltpu.bitcast(a, I32)
    thr = thr_ref[...]
    gt = bits > thr
    eq = bits == thr
    r = lax.broadcasted_iota(I32, (SEL_T, SEL_T), 0)
    c = lax.broadcasted_iota(I32, (SEL_T, SEL_T), 1)
    tri = jnp.where(r >= c, 1.0, 0.0).astype(BF16)
    eq_f = jnp.where(eq, 1.0, 0.0)
    cum_eq = jnp.dot(tri, eq_f.astype(BF16), preferred_element_type=F32) + carry_ref[0:1, :]
    sel = jnp.logical_or(gt, jnp.logical_and(eq, cum_eq <= need_ref[...]))
    sel_f = jnp.where(sel, 1.0, 0.0)
    pos = jnp.dot(tri, sel_f.astype(BF16), preferred_element_type=F32) + carry_ref[1:2, :]
    off_ref[0] = carry_ref[1:2, :].astype(I32)
    carry_ref[0:1, :] = carry_ref[0:1, :] + jnp.sum(eq_f, axis=0, keepdims=True)
    carry_ref[1:2, :] = carry_ref[1:2, :] + jnp.sum(sel_f, axis=0, keepdims=True)
    slot_ref[...] = jnp.where(sel, pos - 1.0, -1.0).astype(I32)
    er = lax.broadcasted_iota(I32, (N_EXPERTS, N_EXPERTS), 0)
    ec = lax.broadcasted_iota(I32, (N_EXPERTS, N_EXPERTS), 1)
    before = jnp.where(er < ec, 1.0, 0.0).astype(BF16)
    rank = jnp.dot(sel_f.astype(BF16), before, preferred_element_type=F32).astype(I32)
    tok = t * SEL_T + lax.broadcasted_iota(I32, (SEL_T, N_EXPERTS), 0)
    tokrank_ref[...] = tok * (1 << RANK_BITS) + rank
    tokcnt_ref[...] = jnp.broadcast_to(jnp.sum(sel_f, axis=1, keepdims=True).astype(I32), (SEL_T, N_EXPERTS))


def _lists_kernel(off_ref, slot_ref, tokrank_ref, aff_ref, idx_ref, gate_ref):
    t = pl.program_id(0)

    @pl.when(t == 0)
    def _():
        idx_ref[...] = jnp.zeros(idx_ref.shape, I32)
        gate_ref[...] = jnp.zeros(gate_ref.shape, F32)

    slot = slot_ref[...]
    tokrank = tokrank_ref[...]
    a = aff_ref[...]
    for e in range(N_EXPERTS):
        base = pl.multiple_of((off_ref[t * N_EXPERTS + e] // LANE) * LANE, LANE)
        lane = base + lax.broadcasted_iota(I32, (1, 2 * LANE), 1)
        match = slot[:, e:e + 1] == lane
        win = (slice(e, e + 1), pl.ds(base, 2 * LANE))
        idx_ref[win] += jnp.sum(jnp.where(match, tokrank[:, e:e + 1], 0), axis=0, keepdims=True)
        gate_ref[win] += jnp.sum(jnp.where(match, a[:, e:e + 1], 0.0), axis=0, keepdims=True)


def _select(aff, cap):
    n_tok = aff.shape[0]
    nt = n_tok // SEL_T
    tile = pl.BlockSpec((SEL_T, N_EXPERTS), lambda t: (t, 0))
    tile_shape = jax.ShapeDtypeStruct((n_tok, N_EXPERTS), I32)
    slot, tokrank, tokcnt, offs = pl.pallas_call(
        functools.partial(_slots_kernel, cap=cap, n_tok=n_tok),
        grid=(nt,),
        in_specs=[pl.BlockSpec((n_tok, N_EXPERTS), lambda t: (0, 0))],
        out_specs=[tile, tile, tile, pl.BlockSpec((1, 1, N_EXPERTS), lambda t: (t, 0, 0))],
        out_shape=[tile_shape, tile_shape, tile_shape, jax.ShapeDtypeStruct((nt, 1, N_EXPERTS), I32)],
        scratch_shapes=[pltpu.VMEM((1, N_EXPERTS), I32), pltpu.VMEM((1, N_EXPERTS), F32),
                        pltpu.VMEM((2, N_EXPERTS), F32)],
        compiler_params=_params(("arbitrary",), 32),
        name="expert_slots",
    )(aff)
    offs = offs.reshape(nt, N_EXPERTS)
    width = cap + 2 * LANE
    tile_p = pl.BlockSpec((SEL_T, N_EXPERTS), lambda t, o: (t, 0))
    idx, gates = pl.pallas_call(
        _lists_kernel,
        grid_spec=pltpu.PrefetchScalarGridSpec(
            num_scalar_prefetch=1, grid=(nt,),
            in_specs=[tile_p, tile_p, tile_p],
            out_specs=[pl.BlockSpec((N_EXPERTS, width), lambda t, o: (0, 0)),
                       pl.BlockSpec((N_EXPERTS, width), lambda t, o: (0, 0))]),
        out_shape=[jax.ShapeDtypeStruct((N_EXPERTS, width), I32),
                   jax.ShapeDtypeStruct((N_EXPERTS, width), F32)],
        compiler_params=_params(("arbitrary",), 32),
        name="expert_lists",
    )(offs.reshape(-1), slot, tokrank, aff)
    return idx[:, :cap], gates[:, :cap], offs, tokcnt


def _gather_kernel(idx_ref, hp_ref, xe_ref, sem, *, rows):
    def row_copy(s):
        tok = idx_ref[0, 0, s] >> RANK_BITS
        return pltpu.make_async_copy(hp_ref.at[pl.ds(tok, 1)], xe_ref.at[0, pl.ds(s, 1)], sem)

    def start(s, carry):
        row_copy(s).start()
        return carry

    def wait(s, carry):
        row_copy(s).wait()
        return carry

    lax.fori_loop(0, rows, start, 0, unroll=GATHER_UNROLL)
    lax.fori_loop(0, rows, wait, 0, unroll=GATHER_UNROLL)


GATHER_UNROLL = 8


def _gather(idx, hp, cap, rows):
    half = hp.shape[1]
    n_blocks = cap // rows
    idx3 = idx.reshape(N_EXPERTS * n_blocks, 1, rows)
    return pl.pallas_call(
        functools.partial(_gather_kernel, rows=rows),
        grid=(N_EXPERTS, n_blocks),
        in_specs=[pl.BlockSpec((1, 1, rows), lambda e, r: (e * n_blocks + r, 0, 0), memory_space=pltpu.SMEM),
                  pl.BlockSpec(memory_space=pl.ANY)],
        out_specs=pl.BlockSpec((1, rows, half), lambda e, r: (e, r, 0)),
        out_shape=jax.ShapeDtypeStruct((N_EXPERTS, cap, half), U32),
        scratch_shapes=[pltpu.SemaphoreType.DMA(())],
        compiler_params=_params(("arbitrary", "arbitrary"), 32),
        name="expert_gather",
    )(idx3, hp)


def _unpack(xp):
    lo = pltpu.bitcast(xp << 16, F32).astype(BF16)
    hi = pltpu.bitcast(xp & jnp.uint32(0xFFFF0000), F32).astype(BF16)
    return lo, hi


def _ffn_up_kernel(xe_hbm, wg_ref, wu_ref, h_ref, xe_vmem, wg_bf, wu_bf, sem, *, tr):
    e, j, r = pl.program_id(0), pl.program_id(1), pl.program_id(2)

    @pl.when(jnp.logical_and(j == 0, r == 0))
    def _():
        copy = pltpu.make_async_copy(xe_hbm.at[e], xe_vmem, sem)
        copy.start()
        copy.wait()

    @pl.when(r == 0)
    def _():
        wg_bf[...] = wg_ref[0].astype(BF16)
        wu_bf[...] = wu_ref[0].astype(BF16)

    lo, hi = _unpack(xe_vmem[pl.ds(pl.multiple_of(r * tr, tr), tr), :])
    half = lo.shape[1]

    def proj(w_bf):
        return (jnp.dot(lo, w_bf[:half, :], preferred_element_type=F32)
                + jnp.dot(hi, w_bf[half:, :], preferred_element_type=F32))

    gate = proj(wg_bf)
    up = proj(wu_bf)
    h_ref[0] = (gate * jax.nn.sigmoid(gate) * up).astype(BF16)


def _ffn_up(xe, w_gate, w_up, tr, tf):
    _, cap, half = xe.shape
    d, f = w_gate.shape[1], w_gate.shape[2]
    return pl.pallas_call(
        functools.partial(_ffn_up_kernel, tr=tr),
        grid=(N_EXPERTS, f // tf, cap // tr),
        in_specs=[pl.BlockSpec(memory_space=pl.ANY),
                  pl.BlockSpec((1, d, tf), lambda e, j, r: (e, 0, j)),
                  pl.BlockSpec((1, d, tf), lambda e, j, r: (e, 0, j))],
        out_specs=pl.BlockSpec((1, tr, tf), lambda e, j, r: (e, r, j)),
        out_shape=jax.ShapeDtypeStruct((N_EXPERTS, cap, f), BF16),
        scratch_shapes=[pltpu.VMEM((cap, half), U32), pltpu.VMEM((d, tf), BF16), pltpu.VMEM((d, tf), BF16),
                        pltpu.SemaphoreType.DMA(())],
        compiler_params=_params(("arbitrary", "arbitrary", "arbitrary"), 56),
        name="expert_ffn_up",
    )(xe, w_gate, w_up)


def _pack_halves(x):
    w = x.shape[1] // 2
    bits = pltpu.bitcast(x.astype(BF16).astype(F32), U32)
    return (bits[:, :w] >> 16) | (bits[:, w:] & jnp.uint32(0xFFFF0000))


def _ffn_down_kernel(h_ref, wd_ref, g_ref, o_ref, wd_bf, *, tr):
    @pl.when(pl.program_id(2) == 0)
    def _():
        wd_bf[...] = wd_ref[0].astype(BF16)

    rows = pl.ds(pl.multiple_of(pl.program_id(2) * tr, tr), tr)
    out = jnp.dot(h_ref[0, rows, :], wd_bf[...], preferred_element_type=F32)
    o_ref[0] = _pack_halves(g_ref[0] * out)


def _ffn_down(hid, w_down, gates, tr, tn):
    _, cap, f = hid.shape
    d = w_down.shape[2]
    return pl.pallas_call(
        functools.partial(_ffn_down_kernel, tr=tr),
        grid=(N_EXPERTS, d // tn, cap // tr),
        in_specs=[pl.BlockSpec((1, cap, f), lambda e, j, r: (e, 0, 0)),
                  pl.BlockSpec((1, f, tn), lambda e, j, r: (e, 0, j)),
                  pl.BlockSpec((1, tr, 1), lambda e, j, r: (e, r, 0))],
        out_specs=pl.BlockSpec((1, tr, tn // 2), lambda e, j, r: (e, r, j)),
        out_shape=jax.ShapeDtypeStruct((N_EXPERTS, cap, d // 2), U32),
        scratch_shapes=[pltpu.VMEM((f, tn), BF16)],
        compiler_params=_params(("parallel", "parallel", "arbitrary"), 48),
        name="expert_ffn_down",
    )(hid, w_down, gates.reshape(N_EXPERTS, cap, 1))


def _combine_kernel(lo_ref, hi_ref, maxc_ref, idx_ref, x_ref, cnt_ref, con_ref, y_ref, stage_ref, sem,
                    *, nt, cap, tn):
    t = pl.program_id(0)

    @pl.when(t == 0)
    def _():
        stage_ref[...] = jnp.zeros(stage_ref.shape, U32)

    def row_copy(row):
        v = idx_ref[row]
        tok = (v >> RANK_BITS) - t * SEL_T
        return pltpu.make_async_copy(con_ref.at[pl.ds(row, 1)],
                                     stage_ref.at[v & ((1 << RANK_BITS) - 1), pl.ds(tok, 1)], sem)

    def for_rows(fn):
        def per_row(row, carry):
            fn(row_copy(row))
            return carry

        def per_expert(e, carry):
            k = e * nt + t
            return lax.fori_loop(e * cap + lo_ref[k], e * cap + hi_ref[k], per_row, carry)

        lax.fori_loop(0, N_EXPERTS, per_expert, 0)

    for_rows(lambda copy: copy.start())
    y_ref[...] = x_ref[...]
    for_rows(lambda copy: copy.wait())

    half = tn // 2
    cnt = jnp.broadcast_to(cnt_ref[:, 0:1], (SEL_T, half))

    def add_rank(k, carry):
        live = k < cnt
        for j in range(y_ref.shape[1] // tn):
            xp = jnp.where(live, stage_ref[k, :, j * half:(j + 1) * half], jnp.uint32(0))
            y_ref[:, j * tn:j * tn + half] += pltpu.bitcast(xp << 16, F32)
            y_ref[:, j * tn + half:(j + 1) * tn] += pltpu.bitcast(xp & jnp.uint32(0xFFFF0000), F32)
        return carry

    lax.fori_loop(0, maxc_ref[t], add_rank, 0)


def _combine(x1, idx, contrib, offs, tokcnt, cap, tn):
    n_tok, d = x1.shape
    nt = n_tok // SEL_T
    lo = jnp.transpose(offs)
    hi = jnp.concatenate([lo[:, 1:], jnp.full((N_EXPERTS, 1), cap, I32)], axis=1)
    maxc = jnp.max(tokcnt[:, 0].reshape(nt, SEL_T), axis=1)
    grid_spec = pltpu.PrefetchScalarGridSpec(
        num_scalar_prefetch=4,
        grid=(nt,),
        in_specs=[
            pl.BlockSpec((SEL_T, d), lambda t, *_: (t, 0)),
            pl.BlockSpec((SEL_T, N_EXPERTS), lambda t, *_: (t, 0)),
            pl.BlockSpec(memory_space=pl.ANY),
        ],
        out_specs=pl.BlockSpec((SEL_T, d), lambda t, *_: (t, 0)),
        scratch_shapes=[pltpu.VMEM((N_EXPERTS, SEL_T, d // 2), U32), pltpu.SemaphoreType.DMA(())],
    )
    return pl.pallas_call(
        functools.partial(_combine_kernel, nt=nt, cap=cap, tn=tn),
        grid_spec=grid_spec,
        out_shape=jax.ShapeDtypeStruct((n_tok, d), F32),
        compiler_params=_params(("arbitrary",), 48),
        name="expert_combine",
    )(lo.reshape(-1), hi.reshape(-1), maxc, idx.reshape(-1), x1, tokcnt,
      contrib.reshape(N_EXPERTS * cap, d // 2))


DOWN_TN = 1024
UP_TF = 256
OUT_TN = 1024


def _tiles(n_tok, n_seq, cap):
    return dict(
        tm=min(512, n_seq),
        rows=min(512, cap),
        tr=min(1024, cap),
    )


def _encoder_layer(x, p):
    b, n, d = x.shape
    n_tok = b * n
    cap = EC_FACTOR * n_tok // N_EXPERTS
    tl = _tiles(n_tok, n, cap)
    x2 = x.reshape(n_tok, d)

    proj, v_b = _in_projection(x2, p["norm1"], p["w_in"], p["gains"], p["cos"][n], p["sin"][n], n, tl["tm"])
    oa = _attention_a(proj, b, n, ATT_TQ)
    ob = _attention_b(proj, v_b, p["sink"], p["bias"], b, n)
    x1 = _out_projection(oa, ob, p["out_norm_a"], p["out_norm_b"], p["w_out"], x2, tl["tm"], OUT_TN)

    hp, aff = _router(x1, p["norm2"], p["w_router"], tl["tm"])
    idx, gates, offs, tokcnt = _select(aff, cap)
    xe = _gather(idx, hp, cap, tl["rows"])
    hid = _ffn_up(xe, p["w_gate"], p["w_up"], tl["tr"], UP_TF)
    contrib = _ffn_down(hid, p["w_down"], gates, tl["tr"], DOWN_TN)
    y = _combine(x1, idx, contrib, offs, tokcnt, cap, DOWN_TN)
    return y.reshape(b, n, d)


def kernel(x_prompt, x_sample, norm1, w_in, q_norm_a, k_norm_a, q_norm_b, k_norm_b, sink_b, rel_bias,
           out_norm_a, out_norm_b, w_out, norm2, w_router, w_gate, w_up, w_down):
    depth = norm1.shape[0]
    seqs = sorted({x_prompt.shape[1], x_sample.shape[1]})
    tables = {n: _rope_tables(n) for n in seqs}
    bias = _window_bias(rel_bias)
    ones = jnp.ones((KVW,), F32)

    layers = []
    for l in range(depth):
        gains = jnp.concatenate([jnp.tile(q_norm_a[l], H_A), jnp.tile(k_norm_a[l], KV_A), ones,
                                 jnp.tile(q_norm_b[l], H_B), jnp.tile(k_norm_b[l], KV_B), ones])
        layers.append(dict(
            norm1=norm1[l], w_in=w_in[l].astype(BF16), gains=gains.reshape(1, IN_WIDTH),
            cos={n: tables[n][0] for n in seqs}, sin={n: tables[n][1] for n in seqs},
            sink=sink_b[l], bias=bias,
            out_norm_a=out_norm_a[l], out_norm_b=out_norm_b[l], w_out=w_out[l].astype(BF16),
            norm2=norm2[l], w_router=w_router[l].astype(BF16),
            w_gate=w_gate[l], w_up=w_up[l], w_down=w_down[l],
        ))

    def run(x):
        for p in layers:
            x = _encoder_layer(x, p)
        return x

    return (run(x_prompt), run(x_sample))
```

```python
import functools
import math

import jax
import jax.numpy as jnp
from jax import lax
from jax.experimental import pallas as pl
from jax.experimental.pallas import tpu as pltpu

F32 = jnp.float32
BF16 = jnp.bfloat16
I32 = jnp.int32
U32 = jnp.uint32

HEAD_DIM = 128
H_A = 16
KV_A = 4
H_B = 16
KV_B = 4
GROUP = 4
W_A = H_A * HEAD_DIM
W_B = H_B * HEAD_DIM
KVW = KV_A * HEAD_DIM
IN_WIDTH = W_A + 2 * KVW + W_B + 2 * KVW
GRID_W = 64
HALF_ROT = HEAD_DIM // 2
ROPE_THETA = 10000.0
REL_BUCKETS = 32
REL_MAX_DIST = 128
WINDOW = 128
N_EXPERTS = 16
EC_FACTOR = 2
NORM_EPS = 1e-6
NEG_INF = -1e30
SCALE = 1.0 / math.sqrt(HEAD_DIM)

V7X_VMEM_BYTES = 64 * 1024 * 1024
LANE = 128


def _vmem(mib):
    assert mib * 1024 * 1024 < V7X_VMEM_BYTES
    return mib * 1024 * 1024


def _params(sem, mib):
    return pltpu.CompilerParams(dimension_semantics=sem, vmem_limit_bytes=_vmem(mib))


IN_TN = 512
_ROPE_TILES = 5
_VA_TILE = 5
_N_TILES = IN_WIDTH // IN_TN
_MAIN_TILES = _N_TILES - 1


def _inproj_kernel(x_ref, g1_ref, w_ref, gain_ref, cos_ref, sin_ref, o_ref, vb_ref, h_ref, raw_ref):
    j = pl.program_id(1)

    @pl.when(j == 0)
    def _():
        x = x_ref[...]
        ms = jnp.mean(x * x, axis=-1, keepdims=True)
        h_ref[...] = (x * lax.rsqrt(ms + NORM_EPS) * g1_ref[...]).astype(BF16)
        raw_ref[...] = jnp.zeros(raw_ref.shape, F32)

    prev = raw_ref[...]
    acc = jnp.dot(h_ref[...], w_ref[...], preferred_element_type=F32)
    jp = j - 1
    is_v = jp == _VA_TILE
    is_rope = jp < _ROPE_TILES
    is_q = jnp.logical_or(jp < W_A // IN_TN,
                          jnp.logical_and(jp > _VA_TILE, jp < (IN_WIDTH - 2 * KVW) // IN_TN))
    outs = []
    for hh in range(IN_TN // HEAD_DIM):
        a = prev[:, hh * HEAD_DIM:(hh + 1) * HEAD_DIM]
        ms = jnp.mean(a * a, axis=-1, keepdims=True)
        r = jnp.where(is_v, 1.0, lax.rsqrt(ms + NORM_EPS))
        outs.append(a * r * gain_ref[:, hh * HEAD_DIM:(hh + 1) * HEAD_DIM])
    y = jnp.concatenate(outs, axis=-1)
    reps = IN_TN // HEAD_DIM
    c = jnp.concatenate([jnp.where(is_rope, cos_ref[...], 1.0)] * reps, axis=-1)
    s = jnp.concatenate([jnp.where(is_rope, sin_ref[...], 0.0)] * reps, axis=-1)
    lane = lax.broadcasted_iota(I32, y.shape, 1)
    first_half = (lane % (2 * (HALF_ROT // 2))) < (HALF_ROT // 2)
    partner = jnp.where(first_half,
                        pltpu.roll(y, IN_TN - HALF_ROT // 2, axis=1),
                        pltpu.roll(y, HALF_ROT // 2, axis=1))
    o_ref[...] = ((y * c + partner * s) * jnp.where(is_q, SCALE * LOG2E, 1.0)).astype(BF16)
    raw_ref[...] = acc

    @pl.when(j == _N_TILES - 1)
    def _():
        vb_ref[...] = acc.astype(BF16)


def _rope_tables(n):
    rows = n // GRID_W
    row = jnp.repeat(jnp.arange(rows, dtype=F32), GRID_W)
    col = jnp.tile(jnp.arange(GRID_W, dtype=F32), rows)
    inv = ROPE_THETA ** (-jnp.arange(0, HALF_ROT, 2, dtype=F32) / HALF_ROT)
    ar, ac = row[:, None] * inv, col[:, None] * inv
    cr, sr, cc, sc = jnp.cos(ar), jnp.sin(ar), jnp.cos(ac), jnp.sin(ac)
    cos_t = jnp.concatenate([cr, cr, cc, cc], axis=-1)
    sin_t = jnp.concatenate([-sr, sr, -sc, sc], axis=-1)
    return cos_t, sin_t


def _in_projection(x2, norm1, w_in_bf, gains, cos_t, sin_t, n_seq, tm):
    n_tok, d = x2.shape
    seq_blocks = n_seq // tm
    return pl.pallas_call(
        _inproj_kernel,
        grid=(n_tok // tm, _N_TILES),
        in_specs=[
            pl.BlockSpec((tm, d), lambda i, j: (i, 0)),
            pl.BlockSpec((1, d), lambda i, j: (0, 0)),
            pl.BlockSpec((d, IN_TN), lambda i, j: (0, j)),
            pl.BlockSpec((1, IN_TN), lambda i, j: (0, jnp.maximum(j - 1, 0))),
            pl.BlockSpec((tm, HEAD_DIM), lambda i, j: (i % seq_blocks, 0)),
            pl.BlockSpec((tm, HEAD_DIM), lambda i, j: (i % seq_blocks, 0)),
        ],
        out_specs=[pl.BlockSpec((tm, IN_TN), lambda i, j: (i, jnp.maximum(j - 1, 0))),
                   pl.BlockSpec((tm, IN_TN), lambda i, j: (i, 0))],
        out_shape=[jax.ShapeDtypeStruct((n_tok, _MAIN_TILES * IN_TN), BF16),
                   jax.ShapeDtypeStruct((n_tok, IN_TN), BF16)],
        scratch_shapes=[pltpu.VMEM((tm, d), BF16), pltpu.VMEM((tm, IN_TN), F32)],
        compiler_params=_params(("parallel", "arbitrary"), 48),
        name="in_projection",
    )(x2, norm1.reshape(1, d), w_in_bf, gains, cos_t, sin_t)


ATT_TQ = 128
ATT_TKC = 256
ATT_UNROLL = 32
LOG2E = 1.4426950408889634


ATT_UNROLL_BOUNDED = 64
ATT_PV_GROUP = 16
ATT_BOUND_LIMIT = 40.0
ATT_BOUND_SLACK = 1.001


def _flash_kernel(q_ref, k_ref, v_ref, o_ref, vt_ref, kmax_ref, *, n_chunks, unroll):
    @pl.when(pl.program_id(2) == 0)
    def _():
        def prepare_chunk(c, kmax2):
            st = pl.multiple_of(c * ATT_TKC, ATT_TKC)
            vt_ref[:, pl.ds(st, ATT_TKC)] = v_ref[pl.ds(st, ATT_TKC), :].T
            kf = k_ref[pl.ds(st, ATT_TKC), :].astype(F32)
            return jnp.maximum(kmax2, jnp.max(jnp.sum(kf * kf, axis=1, keepdims=True), axis=0, keepdims=True))
        kmax2 = lax.fori_loop(0, n_chunks, prepare_chunk, jnp.zeros((1, 1), F32))
        kmax_ref[...] = jnp.broadcast_to(jnp.sqrt(kmax2), kmax_ref.shape)

    q = q_ref[...]
    tq = q.shape[0]
    q_t = jnp.concatenate([q[:, g * HEAD_DIM:(g + 1) * HEAD_DIM].T for g in range(GROUP)], axis=1)
    nq = GROUP * tq
    qf = q_t.astype(F32)
    bound = jnp.sqrt(jnp.sum(qf * qf, axis=0, keepdims=True)) * kmax_ref[...] * ATT_BOUND_SLACK

    def write(acc, l):
        o_t = acc / l
        for g in range(GROUP):
            o_ref[:, g * HEAD_DIM:(g + 1) * HEAD_DIM] = o_t[:, g * tq:(g + 1) * tq].T

    def start(c, u):
        return pl.multiple_of((c * unroll + u) * ATT_TKC, ATT_TKC)

    bounded = jnp.max(bound) <= ATT_BOUND_LIMIT

    @pl.when(bounded)
    def _():
        trip = math.gcd(n_chunks, ATT_UNROLL_BOUNDED)
        group = math.gcd(trip, ATT_PV_GROUP)

        def body(c, carry):
            l, acc = carry
            for g0 in range(0, trip, group):
                first = (c * trip + g0) * ATT_TKC
                probs = []
                for u in range(group):
                    keys = pl.ds(pl.multiple_of(first + u * ATT_TKC, ATT_TKC), ATT_TKC)
                    s = jnp.dot(k_ref[keys, :], q_t, preferred_element_type=F32)
                    p = jnp.exp2(s - bound)
                    l = l + jnp.sum(p, axis=0, keepdims=True)
                    probs.append(p.astype(BF16))
                keys = pl.ds(pl.multiple_of(first, group * ATT_TKC), group * ATT_TKC)
                acc = acc + jnp.dot(vt_ref[:, keys], jnp.concatenate(probs, axis=0), preferred_element_type=F32)
            return l, acc

        init = (jnp.zeros((1, nq), F32), jnp.zeros((HEAD_DIM, nq), F32))
        l, acc = lax.fori_loop(0, n_chunks // trip, body, init)
        write(acc, l)

    @pl.when(jnp.logical_not(bounded))
    def _():
        def body(c, carry):
            m, l, acc = carry
            scores = [jnp.dot(k_ref[pl.ds(start(c, u), ATT_TKC), :], q_t, preferred_element_type=F32)
                      for u in range(unroll)]
            for u, s in enumerate(scores):
                m_new = jnp.maximum(m, jnp.max(s, axis=0, keepdims=True))
                alpha = jnp.exp2(m - m_new)
                p = jnp.exp2(s - m_new)
                l = alpha * l + jnp.sum(p, axis=0, keepdims=True)
                pv = jnp.dot(vt_ref[:, pl.ds(start(c, u), ATT_TKC)], p.astype(BF16), preferred_element_type=F32)
                acc = alpha * acc + pv
                m = m_new
            return m, l, acc

        init = (jnp.full((1, nq), -jnp.inf, F32), jnp.zeros((1, nq), F32), jnp.zeros((HEAD_DIM, nq), F32))
        _, l, acc = lax.fori_loop(0, n_chunks // unroll, body, init)
        write(acc, l)


def _attention_a(proj, b, n, tq):
    n_tok = b * n
    qb = n // tq
    k_col = W_A // HEAD_DIM
    v_col = (W_A + KVW) // HEAD_DIM
    n_chunks = n // ATT_TKC
    unroll = math.gcd(n_chunks, ATT_UNROLL)
    return pl.pallas_call(
        functools.partial(_flash_kernel, n_chunks=n_chunks, unroll=unroll),
        grid=(b, KV_A, qb),
        in_specs=[
            pl.BlockSpec((tq, GROUP * HEAD_DIM), lambda bi, h, qi: (bi * qb + qi, h)),
            pl.BlockSpec((n, HEAD_DIM), lambda bi, h, qi: (bi, k_col + h)),
            pl.BlockSpec((n, HEAD_DIM), lambda bi, h, qi: (bi, v_col + h)),
        ],
        out_specs=pl.BlockSpec((tq, GROUP * HEAD_DIM), lambda bi, h, qi: (bi * qb + qi, h)),
        out_shape=jax.ShapeDtypeStruct((n_tok, W_A), F32),
        scratch_shapes=[pltpu.VMEM((HEAD_DIM, n), BF16), pltpu.VMEM((1, GROUP * tq), F32)],
        compiler_params=_params(("arbitrary", "arbitrary", "arbitrary"), 48),
        name="attention_global",
    )(proj, proj, proj)


WIN_TQ = 2 * WINDOW
WIN_TK = WIN_TQ + 2 * WINDOW


def _t5_buckets(rel):
    nb = REL_BUCKETS // 2
    max_exact = nb // 2
    ret = jnp.where(rel > 0, nb, 0)
    n = jnp.abs(rel)
    nf = jnp.maximum(n, 1).astype(F32)
    large = max_exact + (jnp.log(nf / max_exact) / math.log(REL_MAX_DIST / max_exact)
                         * (nb - max_exact)).astype(I32)
    large = jnp.minimum(large, nb - 1)
    return ret + jnp.where(n < max_exact, n, large)


def _window_bias(rel_bias):
    span = WIN_TQ + WIN_TK
    kk = jnp.arange(span)
    rel = jnp.where(kk < WIN_TK, kk, kk - span) - WINDOW
    tbl = jnp.where((jnp.abs(rel) <= WINDOW)[:, None], rel_bias[_t5_buckets(rel)].astype(F32), NEG_INF)
    flat = jnp.tile(tbl.T, (1, WIN_TQ))[:, :WIN_TQ * (span - 1)]
    bias = flat.reshape(H_B, WIN_TQ, span - 1)[:, :, :WIN_TK]
    j = jnp.arange(WIN_TK)
    before, after = j < WINDOW, j >= WINDOW + WIN_TQ
    masks = jnp.stack([jnp.zeros_like(before), before, after, jnp.logical_or(before, after)])
    bias = jnp.where(masks[:, None, None, :], NEG_INF, bias[None])
    bias = bias.reshape(4, KV_B, GROUP, WIN_TQ, WIN_TK).transpose(0, 1, 4, 2, 3)
    return bias.reshape(4, KV_B, WIN_TK, GROUP * WIN_TQ) * LOG2E


def _window_kernel(sink_ref, q_ref, kp_ref, kc_ref, kn_ref, vp_ref, vc_ref, vn_ref, bias_ref, o_ref):
    h = pl.program_id(0)
    q = q_ref[...]
    q_all = jnp.concatenate([q[:, g * HEAD_DIM:(g + 1) * HEAD_DIM] for g in range(GROUP)], axis=0)
    kcat = jnp.concatenate([kp_ref[...], kc_ref[...], kn_ref[...]], axis=0)
    vcat = jnp.concatenate([vp_ref[...], vc_ref[...], vn_ref[...]], axis=0)
    s = lax.dot_general(kcat, q_all, (((1,), (1,)), ((), ())), preferred_element_type=F32) + bias_ref[0, 0]
    head = lax.broadcasted_iota(I32, (1, GROUP * WIN_TQ), 1) // WIN_TQ
    snk = jnp.zeros((1, GROUP * WIN_TQ), F32)
    for g in range(GROUP):
        snk = jnp.where(head == g, sink_ref[h * GROUP + g] * LOG2E, snk)
    m = jnp.maximum(jnp.max(s, axis=0, keepdims=True), snk)
    e = jnp.exp2(s - m)
    denom = jnp.sum(e, axis=0, keepdims=True) + jnp.exp2(snk - m)
    o_t = lax.dot_general(vcat, e.astype(BF16), (((0,), (0,)), ((), ())), preferred_element_type=F32) / denom
    for g in range(GROUP):
        o_ref[:, g * HEAD_DIM:(g + 1) * HEAD_DIM] = o_t[:, g * WIN_TQ:(g + 1) * WIN_TQ].T


def _attention_b(proj, v_b, sink, bias, b, n):
    n_tok = b * n
    qb = n // WIN_TQ
    nb128 = n // WINDOW
    base = W_A + 2 * KVW
    q_col = base // (GROUP * HEAD_DIM)
    k_col = (base + W_B) // HEAD_DIM
    v_col = 0

    def prev_map(col):
        return lambda h, bi, i, s: (bi * nb128 + jnp.maximum(2 * i - 1, 0), col + h)

    def cur_map(col):
        return lambda h, bi, i, s: (bi * qb + i, col + h)

    def next_map(col):
        return lambda h, bi, i, s: (bi * nb128 + jnp.minimum(2 * i + 2, nb128 - 1), col + h)

    grid_spec = pltpu.PrefetchScalarGridSpec(
        num_scalar_prefetch=1,
        grid=(KV_B, b, qb),
        in_specs=[
            pl.BlockSpec((WIN_TQ, GROUP * HEAD_DIM), lambda h, bi, i, s: (bi * qb + i, q_col + h)),
            pl.BlockSpec((WINDOW, HEAD_DIM), prev_map(k_col)),
            pl.BlockSpec((WIN_TQ, HEAD_DIM), cur_map(k_col)),
            pl.BlockSpec((WINDOW, HEAD_DIM), next_map(k_col)),
            pl.BlockSpec((WINDOW, HEAD_DIM), prev_map(v_col)),
            pl.BlockSpec((WIN_TQ, HEAD_DIM), cur_map(v_col)),
            pl.BlockSpec((WINDOW, HEAD_DIM), next_map(v_col)),
            pl.BlockSpec((1, 1, WIN_TK, GROUP * WIN_TQ),
                         lambda h, bi, i, s: (jnp.where(i == 0, 1, 0) + jnp.where(i == qb - 1, 2, 0), h, 0, 0)),
        ],
        out_specs=pl.BlockSpec((WIN_TQ, GROUP * HEAD_DIM), lambda h, bi, i, s: (bi * qb + i, h)),
    )
    return pl.pallas_call(
        _window_kernel,
        grid_spec=grid_spec,
        out_shape=jax.ShapeDtypeStruct((n_tok, W_B), F32),
        compiler_params=_params(("parallel", "parallel", "parallel"), 32),
        name="attention_window",
    )(sink, proj, proj, proj, proj, v_b, v_b, v_b, bias)


def _outproj_kernel(oa_ref, ob_ref, ga_ref, gb_ref, w_ref, x_ref, o_ref, mix_ref):
    j = pl.program_id(1)

    @pl.when(j == 0)
    def _():
        for src, gain, lo in ((oa_ref, ga_ref, 0), (ob_ref, gb_ref, W_A)):
            o = src[...]
            ms = jnp.mean(o * o, axis=-1, keepdims=True)
            mix_ref[:, lo:lo + o.shape[1]] = (o * lax.rsqrt(ms + NORM_EPS) * gain[...]).astype(BF16)

    o_ref[...] = x_ref[...] + jnp.dot(mix_ref[...], w_ref[...], preferred_element_type=F32)


def _out_projection(oa, ob, gain_a, gain_b, w_out_bf, x2, tm, tn):
    n_tok, d = x2.shape
    return pl.pallas_call(
        _outproj_kernel,
        grid=(n_tok // tm, d // tn),
        in_specs=[
            pl.BlockSpec((tm, W_A), lambda i, j: (i, 0)),
            pl.BlockSpec((tm, W_B), lambda i, j: (i, 0)),
            pl.BlockSpec((1, W_A), lambda i, j: (0, 0)),
            pl.BlockSpec((1, W_B), lambda i, j: (0, 0)),
            pl.BlockSpec((W_A + W_B, tn), lambda i, j: (0, j)),
            pl.BlockSpec((tm, tn), lambda i, j: (i, j)),
        ],
        out_specs=pl.BlockSpec((tm, tn), lambda i, j: (i, j)),
        out_shape=jax.ShapeDtypeStruct((n_tok, d), F32),
        scratch_shapes=[pltpu.VMEM((tm, W_A + W_B), BF16)],
        compiler_params=_params(("parallel", "arbitrary"), 48),
        name="out_projection",
    )(oa, ob, gain_a.reshape(1, W_A), gain_b.reshape(1, W_B), w_out_bf, x2)


def _router_kernel(x_ref, g_ref, wr_ref, hp_ref, aff_ref):
    x = x_ref[...]
    ms = jnp.mean(x * x, axis=-1, keepdims=True)
    hb = (x * lax.rsqrt(ms + NORM_EPS) * g_ref[...]).astype(BF16)
    logits = jnp.dot(hb, wr_ref[...], preferred_element_type=F32)
    e = jnp.exp(logits - jnp.max(logits, axis=-1, keepdims=True))
    aff_ref[...] = e / jnp.sum(e, axis=-1, keepdims=True)
    half = hb.shape[1] // 2
    bits = pltpu.bitcast(hb.astype(F32), U32)
    hp_ref[...] = (bits[:, :half] >> 16) | (bits[:, half:] & jnp.uint32(0xFFFF0000))


def _router(x1, norm2, w_router_bf, tm):
    n_tok, d = x1.shape
    return pl.pallas_call(
        _router_kernel,
        grid=(n_tok // tm,),
        in_specs=[
            pl.BlockSpec((tm, d), lambda i: (i, 0)),
            pl.BlockSpec((1, d), lambda i: (0, 0)),
            pl.BlockSpec((d, N_EXPERTS), lambda i: (0, 0)),
        ],
        out_specs=[pl.BlockSpec((tm, d // 2), lambda i: (i, 0)),
                   pl.BlockSpec((tm, N_EXPERTS), lambda i: (i, 0))],
        out_shape=[jax.ShapeDtypeStruct((n_tok, d // 2), U32),
                   jax.ShapeDtypeStruct((n_tok, N_EXPERTS), F32)],
        compiler_params=_params(("parallel",), 48),
        name="router",
    )(x1, norm2.reshape(1, d), w_router_bf)


SEL_T = 128
SEL_CHUNK = 1024
RANK_BITS = 4
assert N_EXPERTS <= 1 << RANK_BITS


def _slots_kernel(aff_ref, slot_ref, tokrank_ref, tokcnt_ref, off_ref, thr_ref, need_ref, carry_ref, *, cap, n_tok):
    t = pl.program_id(0)
    n_chunks = n_tok // SEL_CHUNK

    def count(pred_fn):
        def body(c, tot):
            bits = pltpu.bitcast(aff_ref[pl.ds(c * SEL_CHUNK, SEL_CHUNK), :], I32)
            return tot + jnp.sum(pred_fn(bits).astype(I32), axis=0, keepdims=True)
        return lax.fori_loop(0, n_chunks, body, jnp.zeros((1, N_EXPERTS), I32))

    @pl.when(t == 0)
    def _():
        def bit_body(b, thr):
            cand = thr | jnp.left_shift(jnp.int32(1), 30 - b)
            return jnp.where(count(lambda bits: bits >= cand) >= cap, cand, thr)
        thr = lax.fori_loop(0, 31, bit_body, jnp.zeros((1, N_EXPERTS), I32))
        thr_ref[...] = thr
        need_ref[...] = (cap - count(lambda bits: bits > thr)).astype(F32)
        carry_ref[...] = jnp.zeros(carry_ref.shape, F32)

    a = aff_ref[pl.ds(t * SEL_T, SEL_T), :]
    bits = pltpu.bitcast(a, I32)
    thr = thr_ref[...]
    gt = bits > thr
    eq = bits == thr
    r = lax.broadcasted_iota(I32, (SEL_T, SEL_T), 0)
    c = lax.broadcasted_iota(I32, (SEL_T, SEL_T), 1)
    tri = jnp.where(r >= c, 1.0, 0.0).astype(BF16)
    eq_f = jnp.where(eq, 1.0, 0.0)
    cum_eq = jnp.dot(tri, eq_f.astype(BF16), preferred_element_type=F32) + carry_ref[0:1, :]
    sel = jnp.logical_or(gt, jnp.logical_and(eq, cum_eq <= need_ref[...]))
    sel_f = jnp.where(sel, 1.0, 0.0)
    pos = jnp.dot(tri, sel_f.astype(BF16), preferred_element_type=F32) + carry_ref[1:2, :]
    off_ref[0] = carry_ref[1:2, :].astype(I32)
    carry_ref[0:1, :] = carry_ref[0:1, :] + jnp.sum(eq_f, axis=0, keepdims=True)
    carry_ref[1:2, :] = carry_ref[1:2, :] + jnp.sum(sel_f, axis=0, keepdims=True)
    slot_ref[...] = jnp.where(sel, pos - 1.0, -1.0).astype(I32)
    er = lax.broadcasted_iota(I32, (N_EXPERTS, N_EXPERTS), 0)
    ec = lax.broadcasted_iota(I32, (N_EXPERTS, N_EXPERTS), 1)
    before = jnp.where(er < ec, 1.0, 0.0).astype(BF16)
    rank = jnp.dot(sel_f.astype(BF16), before, preferred_element_type=F32).astype(I32)
    tok = t * SEL_T + lax.broadcasted_iota(I32, (SEL_T, N_EXPERTS), 0)
    tokrank_ref[...] = tok * (1 << RANK_BITS) + rank
    tokcnt_ref[...] = jnp.broadcast_to(jnp.sum(sel_f, axis=1, keepdims=True).astype(I32), (SEL_T, N_EXPERTS))


def _lists_kernel(off_ref, slot_ref, tokrank_ref, aff_ref, idx_ref, gate_ref):
    t = pl.program_id(0)

    @pl.when(t == 0)
    def _():
        idx_ref[...] = jnp.zeros(idx_ref.shape, I32)
        gate_ref[...] = jnp.zeros(gate_ref.shape, F32)

    slot = slot_ref[...]
    tokrank = tokrank_ref[...]
    a = aff_ref[...]
    for e in range(N_EXPERTS):
        base = pl.multiple_of((off_ref[t * N_EXPERTS + e] // LANE) * LANE, LANE)
        lane = base + lax.broadcasted_iota(I32, (1, 2 * LANE), 1)
        match = slot[:, e:e + 1] == lane
        win = (slice(e, e + 1), pl.ds(base, 2 * LANE))
        idx_ref[win] += jnp.sum(jnp.where(match, tokrank[:, e:e + 1], 0), axis=0, keepdims=True)
        gate_ref[win] += jnp.sum(jnp.where(match, a[:, e:e + 1], 0.0), axis=0, keepdims=True)


def _select(aff, cap):
    n_tok = aff.shape[0]
    nt = n_tok // SEL_T
    tile = pl.BlockSpec((SEL_T, N_EXPERTS), lambda t: (t, 0))
    tile_shape = jax.ShapeDtypeStruct((n_tok, N_EXPERTS), I32)
    slot, tokrank, tokcnt, offs = pl.pallas_call(
        functools.partial(_slots_kernel, cap=cap, n_tok=n_tok),
        grid=(nt,),
        in_specs=[pl.BlockSpec((n_tok, N_EXPERTS), lambda t: (0, 0))],
        out_specs=[tile, tile, tile, pl.BlockSpec((1, 1, N_EXPERTS), lambda t: (t, 0, 0))],
        out_shape=[tile_shape, tile_shape, tile_shape, jax.ShapeDtypeStruct((nt, 1, N_EXPERTS), I32)],
        scratch_shapes=[pltpu.VMEM((1, N_EXPERTS), I32), pltpu.VMEM((1, N_EXPERTS), F32),
                        pltpu.VMEM((2, N_EXPERTS), F32)],
        compiler_params=_params(("arbitrary",), 32),
        name="expert_slots",
    )(aff)
    offs = offs.reshape(nt, N_EXPERTS)
    width = cap + 2 * LANE
    tile_p = pl.BlockSpec((SEL_T, N_EXPERTS), lambda t, o: (t, 0))
    idx, gates = pl.pallas_call(
        _lists_kernel,
        grid_spec=pltpu.PrefetchScalarGridSpec(
            num_scalar_prefetch=1, grid=(nt,),
            in_specs=[tile_p, tile_p, tile_p],
            out_specs=[pl.BlockSpec((N_EXPERTS, width), lambda t, o: (0, 0)),
                       pl.BlockSpec((N_EXPERTS, width), lambda t, o: (0, 0))]),
        out_shape=[jax.ShapeDtypeStruct((N_EXPERTS, width), I32),
                   jax.ShapeDtypeStruct((N_EXPERTS, width), F32)],
        compiler_params=_params(("arbitrary",), 32),
        name="expert_lists",
    )(offs.reshape(-1), slot, tokrank, aff)
    return idx[:, :cap], gates[:, :cap], offs, tokcnt


def _gather_kernel(idx_ref, hp_ref, xe_ref, sem, *, rows):
    def row_copy(s):
        tok = idx_ref[0, 0, s] >> RANK_BITS
        return pltpu.make_async_copy(hp_ref.at[pl.ds(tok, 1)], xe_ref.at[0, pl.ds(s, 1)], sem)

    def start(s, carry):
        row_copy(s).start()
        return carry

    def wait(s, carry):
        row_copy(s).wait()
        return carry

    lax.fori_loop(0, rows, start, 0, unroll=GATHER_UNROLL)
    lax.fori_loop(0, rows, wait, 0, unroll=GATHER_UNROLL)


GATHER_UNROLL = 8


def _gather(idx, hp, cap, rows):
    half = hp.shape[1]
    n_blocks = cap // rows
    idx3 = idx.reshape(N_EXPERTS * n_blocks, 1, rows)
    return pl.pallas_call(
        functools.partial(_gather_kernel, rows=rows),
        grid=(N_EXPERTS, n_blocks),
        in_specs=[pl.BlockSpec((1, 1, rows), lambda e, r: (e * n_blocks + r, 0, 0), memory_space=pltpu.SMEM),
                  pl.BlockSpec(memory_space=pl.ANY)],
        out_specs=pl.BlockSpec((1, rows, half), lambda e, r: (e, r, 0)),
        out_shape=jax.ShapeDtypeStruct((N_EXPERTS, cap, half), U32),
        scratch_shapes=[pltpu.SemaphoreType.DMA(())],
        compiler_params=_params(("arbitrary", "arbitrary"), 32),
        name="expert_gather",
    )(idx3, hp)


def _unpack(xp):
    lo = pltpu.bitcast(xp << 16, F32).astype(BF16)
    hi = pltpu.bitcast(xp & jnp.uint32(0xFFFF0000), F32).astype(BF16)
    return lo, hi


def _ffn_up_kernel(xe_hbm, wg_ref, wu_ref, h_ref, xe_vmem, wg_bf, wu_bf, sem, *, tr):
    e, j, r = pl.program_id(0), pl.program_id(1), pl.program_id(2)

    @pl.when(jnp.logical_and(j == 0, r == 0))
    def _():
        copy = pltpu.make_async_copy(xe_hbm.at[e], xe_vmem, sem)
        copy.start()
        copy.wait()

    @pl.when(r == 0)
    def _():
        wg_bf[...] = wg_ref[0].astype(BF16)
        wu_bf[...] = wu_ref[0].astype(BF16)

    lo, hi = _unpack(xe_vmem[pl.ds(pl.multiple_of(r * tr, tr), tr), :])
    half = lo.shape[1]

    def proj(w_bf):
        return (jnp.dot(lo, w_bf[:half, :], preferred_element_type=F32)
                + jnp.dot(hi, w_bf[half:, :], preferred_element_type=F32))

    gate = proj(wg_bf)
    up = proj(wu_bf)
    h_ref[0] = (gate * jax.nn.sigmoid(gate) * up).astype(BF16)


def _ffn_up(xe, w_gate, w_up, tr, tf):
    _, cap, half = xe.shape
    d, f = w_gate.shape[1], w_gate.shape[2]
    return pl.pallas_call(
        functools.partial(_ffn_up_kernel, tr=tr),
        grid=(N_EXPERTS, f // tf, cap // tr),
        in_specs=[pl.BlockSpec(memory_space=pl.ANY),
                  pl.BlockSpec((1, d, tf), lambda e, j, r: (e, 0, j)),
                  pl.BlockSpec((1, d, tf), lambda e, j, r: (e, 0, j))],
        out_specs=pl.BlockSpec((1, tr, tf), lambda e, j, r: (e, r, j)),
        out_shape=jax.ShapeDtypeStruct((N_EXPERTS, cap, f), BF16),
        scratch_shapes=[pltpu.VMEM((cap, half), U32), pltpu.VMEM((d, tf), BF16), pltpu.VMEM((d, tf), BF16),
                        pltpu.SemaphoreType.DMA(())],
        compiler_params=_params(("arbitrary", "arbitrary", "arbitrary"), 56),
        name="expert_ffn_up",
    )(xe, w_gate, w_up)


def _pack_halves(x):
    w = x.shape[1] // 2
    bits = pltpu.bitcast(x.astype(BF16).astype(F32), U32)
    return (bits[:, :w] >> 16) | (bits[:, w:] & jnp.uint32(0xFFFF0000))


def _ffn_down_kernel(h_ref, wd_ref, g_ref, o_ref, wd_bf, *, tr):
    @pl.when(pl.program_id(2) == 0)
    def _():
        wd_bf[...] = wd_ref[0].astype(BF16)

    rows = pl.ds(pl.multiple_of(pl.program_id(2) * tr, tr), tr)
    out = jnp.dot(h_ref[0, rows, :], wd_bf[...], preferred_element_type=F32)
    o_ref[0] = _pack_halves(g_ref[0] * out)


def _ffn_down(hid, w_down, gates, tr, tn):
    _, cap, f = hid.shape
    d = w_down.shape[2]
    return pl.pallas_call(
        functools.partial(_ffn_down_kernel, tr=tr),
        grid=(N_EXPERTS, d // tn, cap // tr),
        in_specs=[pl.BlockSpec((1, cap, f), lambda e, j, r: (e, 0, 0)),
                  pl.BlockSpec((1, f, tn), lambda e, j, r: (e, 0, j)),
                  pl.BlockSpec((1, tr, 1), lambda e, j, r: (e, r, 0))],
        out_specs=pl.BlockSpec((1, tr, tn // 2), lambda e, j, r: (e, r, j)),
        out_shape=jax.ShapeDtypeStruct((N_EXPERTS, cap, d // 2), U32),
        scratch_shapes=[pltpu.VMEM((f, tn), BF16)],
        compiler_params=_params(("parallel", "parallel", "arbitrary"), 48),
        name="expert_ffn_down",
    )(hid, w_down, gates.reshape(N_EXPERTS, cap, 1))


def _combine_kernel(lo_ref, hi_ref, maxc_ref, idx_ref, x_ref, cnt_ref, con_ref, y_ref, stage_ref, sem,
                    *, nt, cap, tn):
    t = pl.program_id(0)

    @pl.when(t == 0)
    def _():
        stage_ref[...] = jnp.zeros(stage_ref.shape, U32)

    def row_copy(row):
        v = idx_ref[row]
        tok = (v >> RANK_BITS) - t * SEL_T
        return pltpu.make_async_copy(con_ref.at[pl.ds(row, 1)],
                                     stage_ref.at[v & ((1 << RANK_BITS) - 1), pl.ds(tok, 1)], sem)

    def for_rows(fn):
        def per_row(row, carry):
            fn(row_copy(row))
            return carry

        def per_expert(e, carry):
            k = e * nt + t
            return lax.fori_loop(e * cap + lo_ref[k], e * cap + hi_ref[k], per_row, carry)

        lax.fori_loop(0, N_EXPERTS, per_expert, 0)

    for_rows(lambda copy: copy.start())
    y_ref[...] = x_ref[...]
    for_rows(lambda copy: copy.wait())

    half = tn // 2
    cnt = jnp.broadcast_to(cnt_ref[:, 0:1], (SEL_T, half))

    def add_rank(k, carry):
        live = k < cnt
        for j in range(y_ref.shape[1] // tn):
            xp = jnp.where(live, stage_ref[k, :, j * half:(j + 1) * half], jnp.uint32(0))
            y_ref[:, j * tn:j * tn + half] += pltpu.bitcast(xp << 16, F32)
            y_ref[:, j * tn + half:(j + 1) * tn] += pltpu.bitcast(xp & jnp.uint32(0xFFFF0000), F32)
        return carry

    lax.fori_loop(0, maxc_ref[t], add_rank, 0)


def _combine(x1, idx, contrib, offs, tokcnt, cap, tn):
    n_tok, d = x1.shape
    nt = n_tok // SEL_T
    lo = jnp.transpose(offs)
    hi = jnp.concatenate([lo[:, 1:], jnp.full((N_EXPERTS, 1), cap, I32)], axis=1)
    maxc = jnp.max(tokcnt[:, 0].reshape(nt, SEL_T), axis=1)
    grid_spec = pltpu.PrefetchScalarGridSpec(
        num_scalar_prefetch=4,
        grid=(nt,),
        in_specs=[
            pl.BlockSpec((SEL_T, d), lambda t, *_: (t, 0)),
            pl.BlockSpec((SEL_T, N_EXPERTS), lambda t, *_: (t, 0)),
            pl.BlockSpec(memory_space=pl.ANY),
        ],
        out_specs=pl.BlockSpec((SEL_T, d), lambda t, *_: (t, 0)),
        scratch_shapes=[pltpu.VMEM((N_EXPERTS, SEL_T, d // 2), U32), pltpu.SemaphoreType.DMA(())],
    )
    return pl.pallas_call(
        functools.partial(_combine_kernel, nt=nt, cap=cap, tn=tn),
        grid_spec=grid_spec,
        out_shape=jax.ShapeDtypeStruct((n_tok, d), F32),
        compiler_params=_params(("arbitrary",), 48),
        name="expert_combine",
    )(lo.reshape(-1), hi.reshape(-1), maxc, idx.reshape(-1), x1, tokcnt,
      contrib.reshape(N_EXPERTS * cap, d // 2))


DOWN_TN = 1024
UP_TF = 256
OUT_TN = 1024


def _tiles(n_tok, n_seq, cap):
    return dict(
        tm=min(512, n_seq),
        rows=min(512, cap),
        tr=min(1024, cap),
    )


def _encoder_layer(x, p):
    b, n, d = x.shape
    n_tok = b * n
    cap = EC_FACTOR * n_tok // N_EXPERTS
    tl = _tiles(n_tok, n, cap)
    x2 = x.reshape(n_tok, d)

    proj, v_b = _in_projection(x2, p["norm1"], p["w_in"], p["gains"], p["cos"][n], p["sin"][n], n, tl["tm"])
    oa = _attention_a(proj, b, n, ATT_TQ)
    ob = _attention_b(proj, v_b, p["sink"], p["bias"], b, n)
    x1 = _out_projection(oa, ob, p["out_norm_a"], p["out_norm_b"], p["w_out"], x2, tl["tm"], OUT_TN)

    hp, aff = _router(x1, p["norm2"], p["w_router"], tl["tm"])
    idx, gates, offs, tokcnt = _select(aff, cap)
    xe = _gather(idx, hp, cap, tl["rows"])
    hid = _ffn_up(xe, p["w_gate"], p["w_up"], tl["tr"], UP_TF)
    contrib = _ffn_down(hid, p["w_down"], gates, tl["tr"], DOWN_TN)
    y = _combine(x1, idx, contrib, offs, tokcnt, cap, DOWN_TN)
    return y.reshape(b, n, d)


def kernel(x_prompt, x_sample, norm1, w_in, q_norm_a, k_norm_a, q_norm_b, k_norm_b, sink_b, rel_bias,
           out_norm_a, out_norm_b, w_out, norm2, w_router, w_gate, w_up, w_down):
    depth = norm1.shape[0]
    seqs = sorted({x_prompt.shape[1], x_sample.shape[1]})
    tables = {n: _rope_tables(n) for n in seqs}
    bias = _window_bias(rel_bias)
    ones = jnp.ones((KVW,), F32)

    layers = []
    for l in range(depth):
        gains = jnp.concatenate([jnp.tile(q_norm_a[l], H_A), jnp.tile(k_norm_a[l], KV_A), ones,
                                 jnp.tile(q_norm_b[l], H_B), jnp.tile(k_norm_b[l], KV_B), ones])
        layers.append(dict(
            norm1=norm1[l], w_in=w_in[l].astype(BF16), gains=gains.reshape(1, IN_WIDTH),
            cos={n: tables[n][0] for n in seqs}, sin={n: tables[n][1] for n in seqs},
            sink=sink_b[l], bias=bias,
            out_norm_a=out_norm_a[l], out_norm_b=out_norm_b[l], w_out=w_out[l].astype(BF16),
            norm2=norm2[l], w_router=w_router[l].astype(BF16),
            w_gate=w_gate[l], w_up=w_up[l], w_down=w_down[l],
        ))

    def run(x):
        for p in layers:
            x = _encoder_layer(x, p)
        return x

    return (run(x_prompt), run(x_sample))
```

```python
import functools
import math

import jax
import jax.numpy as jnp
from jax import lax
from jax.experimental import pallas as pl
from jax.experimental.pallas import tpu as pltpu

F32 = jnp.float32
BF16 = jnp.bfloat16
I32 = jnp.int32
U32 = jnp.uint32

HEAD_DIM = 128
H_A = 16
KV_A = 4
H_B = 16
KV_B = 4
GROUP = 4
W_A = H_A * HEAD_DIM
W_B = H_B * HEAD_DIM
KVW = KV_A * HEAD_DIM
IN_WIDTH = W_A + 2 * KVW + W_B + 2 * KVW
GRID_W = 64
HALF_ROT = HEAD_DIM // 2
ROPE_THETA = 10000.0
REL_BUCKETS = 32
REL_MAX_DIST = 128
WINDOW = 128
N_EXPERTS = 16
EC_FACTOR = 2
NORM_EPS = 1e-6
NEG_INF = -1e30
SCALE = 1.0 / math.sqrt(HEAD_DIM)

V7X_VMEM_BYTES = 64 * 1024 * 1024
LANE = 128


def _vmem(mib):
    assert mib * 1024 * 1024 < V7X_VMEM_BYTES
    return mib * 1024 * 1024


def _params(sem, mib):
    return pltpu.CompilerParams(dimension_semantics=sem, vmem_limit_bytes=_vmem(mib))


IN_TN = 512
_ROPE_TILES = 5
_VA_TILE = 5
_N_TILES = IN_WIDTH // IN_TN
_MAIN_TILES = _N_TILES - 1


def _inproj_kernel(x_ref, g1_ref, w_ref, gain_ref, cos_ref, sin_ref, o_ref, vb_ref, h_ref, raw_ref):
    j = pl.program_id(1)

    @pl.when(j == 0)
    def _():
        x = x_ref[...]
        ms = jnp.mean(x * x, axis=-1, keepdims=True)
        h_ref[...] = (x * lax.rsqrt(ms + NORM_EPS) * g1_ref[...]).astype(BF16)
        raw_ref[...] = jnp.zeros(raw_ref.shape, F32)

    prev = raw_ref[...]
    acc = jnp.dot(h_ref[...], w_ref[...], preferred_element_type=F32)
    jp = j - 1
    is_v = jp == _VA_TILE
    is_rope = jp < _ROPE_TILES
    is_q = jnp.logical_or(jp < W_A // IN_TN,
                          jnp.logical_and(jp > _VA_TILE, jp < (IN_WIDTH - 2 * KVW) // IN_TN))
    outs = []
    for hh in range(IN_TN // HEAD_DIM):
        a = prev[:, hh * HEAD_DIM:(hh + 1) * HEAD_DIM]
        ms = jnp.mean(a * a, axis=-1, keepdims=True)
        r = jnp.where(is_v, 1.0, lax.rsqrt(ms + NORM_EPS))
        outs.append(a * r * gain_ref[:, hh * HEAD_DIM:(hh + 1) * HEAD_DIM])
    y = jnp.concatenate(outs, axis=-1)
    reps = IN_TN // HEAD_DIM
    c = jnp.concatenate([jnp.where(is_rope, cos_ref[...], 1.0)] * reps, axis=-1)
    s = jnp.concatenate([jnp.where(is_rope, sin_ref[...], 0.0)] * reps, axis=-1)
    lane = lax.broadcasted_iota(I32, y.shape, 1)
    first_half = (lane % (2 * (HALF_ROT // 2))) < (HALF_ROT // 2)
    partner = jnp.where(first_half,
                        pltpu.roll(y, IN_TN - HALF_ROT // 2, axis=1),
                        pltpu.roll(y, HALF_ROT // 2, axis=1))
    o_ref[...] = ((y * c + partner * s) * jnp.where(is_q, SCALE * LOG2E, 1.0)).astype(BF16)
    raw_ref[...] = acc

    @pl.when(j == _N_TILES - 1)
    def _():
        vb_ref[...] = acc.astype(BF16)


def _rope_tables(n):
    rows = n // GRID_W
    row = jnp.repeat(jnp.arange(rows, dtype=F32), GRID_W)
    col = jnp.tile(jnp.arange(GRID_W, dtype=F32), rows)
    inv = ROPE_THETA ** (-jnp.arange(0, HALF_ROT, 2, dtype=F32) / HALF_ROT)
    ar, ac = row[:, None] * inv, col[:, None] * inv
    cr, sr, cc, sc = jnp.cos(ar), jnp.sin(ar), jnp.cos(ac), jnp.sin(ac)
    cos_t = jnp.concatenate([cr, cr, cc, cc], axis=-1)
    sin_t = jnp.concatenate([-sr, sr, -sc, sc], axis=-1)
    return cos_t, sin_t


def _in_projection(x2, norm1, w_in_bf, gains, cos_t, sin_t, n_seq, tm):
    n_tok, d = x2.shape
    seq_blocks = n_seq // tm
    return pl.pallas_call(
        _inproj_kernel,
        grid=(n_tok // tm, _N_TILES),
        in_specs=[
            pl.BlockSpec((tm, d), lambda i, j: (i, 0)),
            pl.BlockSpec((1, d), lambda i, j: (0, 0)),
            pl.BlockSpec((d, IN_TN), lambda i, j: (0, j)),
            pl.BlockSpec((1, IN_TN), lambda i, j: (0, jnp.maximum(j - 1, 0))),
            pl.BlockSpec((tm, HEAD_DIM), lambda i, j: (i % seq_blocks, 0)),
            pl.BlockSpec((tm, HEAD_DIM), lambda i, j: (i % seq_blocks, 0)),
        ],
        out_specs=[pl.BlockSpec((tm, IN_TN), lambda i, j: (i, jnp.maximum(j - 1, 0))),
                   pl.BlockSpec((tm, IN_TN), lambda i, j: (i, 0))],
        out_shape=[jax.ShapeDtypeStruct((n_tok, _MAIN_TILES * IN_TN), BF16),
                   jax.ShapeDtypeStruct((n_tok, IN_TN), BF16)],
        scratch_shapes=[pltpu.VMEM((tm, d), BF16), pltpu.VMEM((tm, IN_TN), F32)],
        compiler_params=_params(("parallel", "arbitrary"), 48),
        name="in_projection",
    )(x2, norm1.reshape(1, d), w_in_bf, gains, cos_t, sin_t)


ATT_TQ = 128
ATT_TKC = 256
ATT_UNROLL = 32
LOG2E = 1.4426950408889634


ATT_UNROLL_BOUNDED = 64
ATT_PV_GROUP = 16
ATT_BOUND_LIMIT = 40.0
ATT_BOUND_SLACK = 1.001


def _flash_kernel(q_ref, k_ref, v_ref, o_ref, vt_ref, kmax_ref, *, n_chunks, unroll):
    @pl.when(pl.program_id(2) == 0)
    def _():
        def prepare_chunk(c, kmax2):
            st = pl.multiple_of(c * ATT_TKC, ATT_TKC)
            vt_ref[:, pl.ds(st, ATT_TKC)] = v_ref[pl.ds(st, ATT_TKC), :].T
            kf = k_ref[pl.ds(st, ATT_TKC), :].astype(F32)
            return jnp.maximum(kmax2, jnp.max(jnp.sum(kf * kf, axis=1, keepdims=True), axis=0, keepdims=True))
        kmax2 = lax.fori_loop(0, n_chunks, prepare_chunk, jnp.zeros((1, 1), F32))
        kmax_ref[...] = jnp.broadcast_to(jnp.sqrt(kmax2), kmax_ref.shape)

    q = q_ref[...]
    tq = q.shape[0]
    q_t = jnp.concatenate([q[:, g * HEAD_DIM:(g + 1) * HEAD_DIM].T for g in range(GROUP)], axis=1)
    nq = GROUP * tq
    qf = q_t.astype(F32)
    bound = jnp.sqrt(jnp.sum(qf * qf, axis=0, keepdims=True)) * kmax_ref[...] * ATT_BOUND_SLACK

    def write(acc, l):
        o_t = acc / l
        for g in range(GROUP):
            o_ref[:, g * HEAD_DIM:(g + 1) * HEAD_DIM] = o_t[:, g * tq:(g + 1) * tq].T

    def start(c, u):
        return pl.multiple_of((c * unroll + u) * ATT_TKC, ATT_TKC)

    bounded = jnp.max(bound) <= ATT_BOUND_LIMIT

    @pl.when(bounded)
    def _():
        trip = math.gcd(n_chunks, ATT_UNROLL_BOUNDED)
        group = math.gcd(trip, ATT_PV_GROUP)

        def body(c, carry):
            l, acc = carry
            for g0 in range(0, trip, group):
                first = (c * trip + g0) * ATT_TKC
                probs = []
                for u in range(group):
                    keys = pl.ds(pl.multiple_of(first + u * ATT_TKC, ATT_TKC), ATT_TKC)
                    s = jnp.dot(k_ref[keys, :], q_t, preferred_element_type=F32)
                    p = jnp.exp2(s - bound)
                    l = l + jnp.sum(p, axis=0, keepdims=True)
                    probs.append(p.astype(BF16))
                keys = pl.ds(pl.multiple_of(first, group * ATT_TKC), group * ATT_TKC)
                acc = acc + jnp.dot(vt_ref[:, keys], jnp.concatenate(probs, axis=0), preferred_element_type=F32)
            return l, acc

        init = (jnp.zeros((1, nq), F32), jnp.zeros((HEAD_DIM, nq), F32))
        l, acc = lax.fori_loop(0, n_chunks // trip, body, init)
        write(acc, l)

    @pl.when(jnp.logical_not(bounded))
    def _():
        def body(c, carry):
            m, l, acc = carry
            scores = [jnp.dot(k_ref[pl.ds(start(c, u), ATT_TKC), :], q_t, preferred_element_type=F32)
                      for u in range(unroll)]
            for u, s in enumerate(scores):
                m_new = jnp.maximum(m, jnp.max(s, axis=0, keepdims=True))
                alpha = jnp.exp2(m - m_new)
                p = jnp.exp2(s - m_new)
                l = alpha * l + jnp.sum(p, axis=0, keepdims=True)
                pv = jnp.dot(vt_ref[:, pl.ds(start(c, u), ATT_TKC)], p.astype(BF16), preferred_element_type=F32)
                acc = alpha * acc + pv
                m = m_new
            return m, l, acc

        init = (jnp.full((1, nq), -jnp.inf, F32), jnp.zeros((1, nq), F32), jnp.zeros((HEAD_DIM, nq), F32))
        _, l, acc = lax.fori_loop(0, n_chunks // unroll, body, init)
        write(acc, l)


def _attention_a(proj, b, n, tq):
    n_tok = b * n
    qb = n // tq
    k_col = W_A // HEAD_DIM
    v_col = (W_A + KVW) // HEAD_DIM
    n_chunks = n // ATT_TKC
    unroll = math.gcd(n_chunks, ATT_UNROLL)
    return pl.pallas_call(
        functools.partial(_flash_kernel, n_chunks=n_chunks, unroll=unroll),
        grid=(b, KV_A, qb),
        in_specs=[
            pl.BlockSpec((tq, GROUP * HEAD_DIM), lambda bi, h, qi: (bi * qb + qi, h)),
            pl.BlockSpec((n, HEAD_DIM), lambda bi, h, qi: (bi, k_col + h)),
            pl.BlockSpec((n, HEAD_DIM), lambda bi, h, qi: (bi, v_col + h)),
        ],
        out_specs=pl.BlockSpec((tq, GROUP * HEAD_DIM), lambda bi, h, qi: (bi * qb + qi, h)),
        out_shape=jax.ShapeDtypeStruct((n_tok, W_A), F32),
        scratch_shapes=[pltpu.VMEM((HEAD_DIM, n), BF16), pltpu.VMEM((1, GROUP * tq), F32)],
        compiler_params=_params(("arbitrary", "arbitrary", "arbitrary"), 48),
        name="attention_global",
    )(proj, proj, proj)


WIN_TQ = 2 * WINDOW
WIN_TK = WIN_TQ + 2 * WINDOW


def _t5_buckets(rel):
    nb = REL_BUCKETS // 2
    max_exact = nb // 2
    ret = jnp.where(rel > 0, nb, 0)
    n = jnp.abs(rel)
    nf = jnp.maximum(n, 1).astype(F32)
    large = max_exact + (jnp.log(nf / max_exact) / math.log(REL_MAX_DIST / max_exact)
                         * (nb - max_exact)).astype(I32)
    large = jnp.minimum(large, nb - 1)
    return ret + jnp.where(n < max_exact, n, large)


def _window_bias(rel_bias):
    span = WIN_TQ + WIN_TK
    kk = jnp.arange(span)
    rel = jnp.where(kk < WIN_TK, kk, kk - span) - WINDOW
    tbl = jnp.where((jnp.abs(rel) <= WINDOW)[:, None], rel_bias[_t5_buckets(rel)].astype(F32), NEG_INF)
    flat = jnp.tile(tbl.T, (1, WIN_TQ))[:, :WIN_TQ * (span - 1)]
    bias = flat.reshape(H_B, WIN_TQ, span - 1)[:, :, :WIN_TK]
    j = jnp.arange(WIN_TK)
    before, after = j < WINDOW, j >= WINDOW + WIN_TQ
    masks = jnp.stack([jnp.zeros_like(before), before, after, jnp.logical_or(before, after)])
    bias = jnp.where(masks[:, None, None, :], NEG_INF, bias[None])
    bias = bias.reshape(4, KV_B, GROUP, WIN_TQ, WIN_TK).transpose(0, 1, 4, 2, 3)
    return bias.reshape(4, KV_B, WIN_TK, GROUP * WIN_TQ) * LOG2E


def _window_kernel(sink_ref, q_ref, kp_ref, kc_ref, kn_ref, vp_ref, vc_ref, vn_ref, bias_ref, o_ref):
    h = pl.program_id(0)
    q = q_ref[...]
    q_all = jnp.concatenate([q[:, g * HEAD_DIM:(g + 1) * HEAD_DIM] for g in range(GROUP)], axis=0)
    kcat = jnp.concatenate([kp_ref[...], kc_ref[...], kn_ref[...]], axis=0)
    vcat = jnp.concatenate([vp_ref[...], vc_ref[...], vn_ref[...]], axis=0)
    s = lax.dot_general(kcat, q_all, (((1,), (1,)), ((), ())), preferred_element_type=F32) + bias_ref[0, 0]
    head = lax.broadcasted_iota(I32, (1, GROUP * WIN_TQ), 1) // WIN_TQ
    snk = jnp.zeros((1, GROUP * WIN_TQ), F32)
    for g in range(GROUP):
        snk = jnp.where(head == g, sink_ref[h * GROUP + g] * LOG2E, snk)
    m = jnp.maximum(jnp.max(s, axis=0, keepdims=True), snk)
    e = jnp.exp2(s - m)
    denom = jnp.sum(e, axis=0, keepdims=True) + jnp.exp2(snk - m)
    o_t = lax.dot_general(vcat, e.astype(BF16), (((0,), (0,)), ((), ())), preferred_element_type=F32) / denom
    for g in range(GROUP):
        o_ref[:, g * HEAD_DIM:(g + 1) * HEAD_DIM] = o_t[:, g * WIN_TQ:(g + 1) * WIN_TQ].T


def _attention_b(proj, v_b, sink, bias, b, n):
    n_tok = b * n
    qb = n // WIN_TQ
    nb128 = n // WINDOW
    base = W_A + 2 * KVW
    q_col = base // (GROUP * HEAD_DIM)
    k_col = (base + W_B) // HEAD_DIM
    v_col = 0

    def prev_map(col):
        return lambda h, bi, i, s: (bi * nb128 + jnp.maximum(2 * i - 1, 0), col + h)

    def cur_map(col):
        return lambda h, bi, i, s: (bi * qb + i, col + h)

    def next_map(col):
        return lambda h, bi, i, s: (bi * nb128 + jnp.minimum(2 * i + 2, nb128 - 1), col + h)

    grid_spec = pltpu.PrefetchScalarGridSpec(
        num_scalar_prefetch=1,
        grid=(KV_B, b, qb),
        in_specs=[
            pl.BlockSpec((WIN_TQ, GROUP * HEAD_DIM), lambda h, bi, i, s: (bi * qb + i, q_col + h)),
            pl.BlockSpec((WINDOW, HEAD_DIM), prev_map(k_col)),
            pl.BlockSpec((WIN_TQ, HEAD_DIM), cur_map(k_col)),
            pl.BlockSpec((WINDOW, HEAD_DIM), next_map(k_col)),
            pl.BlockSpec((WINDOW, HEAD_DIM), prev_map(v_col)),
            pl.BlockSpec((WIN_TQ, HEAD_DIM), cur_map(v_col)),
            pl.BlockSpec((WINDOW, HEAD_DIM), next_map(v_col)),
            pl.BlockSpec((1, 1, WIN_TK, GROUP * WIN_TQ),
                         lambda h, bi, i, s: (jnp.where(i == 0, 1, 0) + jnp.where(i == qb - 1, 2, 0), h, 0, 0)),
        ],
        out_specs=pl.BlockSpec((WIN_TQ, GROUP * HEAD_DIM), lambda h, bi, i, s: (bi * qb + i, h)),
    )
    return pl.pallas_call(
        _window_kernel,
        grid_spec=grid_spec,
        out_shape=jax.ShapeDtypeStruct((n_tok, W_B), F32),
        compiler_params=_params(("parallel", "parallel", "parallel"), 32),
        name="attention_window",
    )(sink, proj, proj, proj, proj, v_b, v_b, v_b, bias)


def _outproj_kernel(oa_ref, ob_ref, ga_ref, gb_ref, w_ref, x_ref, o_ref, mix_ref):
    j = pl.program_id(1)

    @pl.when(j == 0)
    def _():
        for src, gain, lo in ((oa_ref, ga_ref, 0), (ob_ref, gb_ref, W_A)):
            o = src[...]
            ms = jnp.mean(o * o, axis=-1, keepdims=True)
            mix_ref[:, lo:lo + o.shape[1]] = (o * lax.rsqrt(ms + NORM_EPS) * gain[...]).astype(BF16)

    o_ref[...] = x_ref[...] + jnp.dot(mix_ref[...], w_ref[...], preferred_element_type=F32)


def _out_projection(oa, ob, gain_a, gain_b, w_out_bf, x2, tm, tn):
    n_tok, d = x2.shape
    return pl.pallas_call(
        _outproj_kernel,
        grid=(n_tok // tm, d // tn),
        in_specs=[
            pl.BlockSpec((tm, W_A), lambda i, j: (i, 0)),
            pl.BlockSpec((tm, W_B), lambda i, j: (i, 0)),
            pl.BlockSpec((1, W_A), lambda i, j: (0, 0)),
            pl.BlockSpec((1, W_B), lambda i, j: (0, 0)),
            pl.BlockSpec((W_A + W_B, tn), lambda i, j: (0, j)),
            pl.BlockSpec((tm, tn), lambda i, j: (i, j)),
        ],
        out_specs=pl.BlockSpec((tm, tn), lambda i, j: (i, j)),
        out_shape=jax.ShapeDtypeStruct((n_tok, d), F32),
        scratch_shapes=[pltpu.VMEM((tm, W_A + W_B), BF16)],
        compiler_params=_params(("parallel", "arbitrary"), 48),
        name="out_projection",
    )(oa, ob, gain_a.reshape(1, W_A), gain_b.reshape(1, W_B), w_out_bf, x2)


def _router_kernel(x_ref, g_ref, wr_ref, hp_ref, aff_ref):
    x = x_ref[...]
    ms = jnp.mean(x * x, axis=-1, keepdims=True)
    hb = (x * lax.rsqrt(ms + NORM_EPS) * g_ref[...]).astype(BF16)
    logits = jnp.dot(hb, wr_ref[...], preferred_element_type=F32)
    e = jnp.exp(logits - jnp.max(logits, axis=-1, keepdims=True))
    aff_ref[...] = e / jnp.sum(e, axis=-1, keepdims=True)
    half = hb.shape[1] // 2
    bits = pltpu.bitcast(hb.astype(F32), U32)
    hp_ref[...] = (bits[:, :half] >> 16) | (bits[:, half:] & jnp.uint32(0xFFFF0000))


def _router(x1, norm2, w_router_bf, tm):
    n_tok, d = x1.shape
    return pl.pallas_call(
        _router_kernel,
        grid=(n_tok // tm,),
        in_specs=[
            pl.BlockSpec((tm, d), lambda i: (i, 0)),
            pl.BlockSpec((1, d), lambda i: (0, 0)),
            pl.BlockSpec((d, N_EXPERTS), lambda i: (0, 0)),
        ],
        out_specs=[pl.BlockSpec((tm, d // 2), lambda i: (i, 0)),
                   pl.BlockSpec((tm, N_EXPERTS), lambda i: (i, 0))],
        out_shape=[jax.ShapeDtypeStruct((n_tok, d // 2), U32),
                   jax.ShapeDtypeStruct((n_tok, N_EXPERTS), F32)],
        compiler_params=_params(("parallel",), 48),
        name="router",
    )(x1, norm2.reshape(1, d), w_router_bf)


SEL_T = 128
SEL_CHUNK = 1024
RANK_BITS = 4
assert N_EXPERTS <= 1 << RANK_BITS


def _slots_kernel(aff_ref, slot_ref, tokrank_ref, tokcnt_ref, off_ref, thr_ref, need_ref, carry_ref, *, cap, n_tok):
    t = pl.program_id(0)
    n_chunks = n_tok // SEL_CHUNK

    def count(pred_fn):
        def body(c, tot):
            bits = pltpu.bitcast(aff_ref[pl.ds(c * SEL_CHUNK, SEL_CHUNK), :], I32)
            return tot + jnp.sum(pred_fn(bits).astype(I32), axis=0, keepdims=True)
        return lax.fori_loop(0, n_chunks, body, jnp.zeros((1, N_EXPERTS), I32))

    @pl.when(t == 0)
    def _():
        def bit_body(b, thr):
            cand = thr | jnp.left_shift(jnp.int32(1), 30 - b)
            return jnp.where(count(lambda bits: bits >= cand) >= cap, cand, thr)
        thr = lax.fori_loop(0, 31, bit_body, jnp.zeros((1, N_EXPERTS), I32))
        thr_ref[...] = thr
        need_ref[...] = (cap - count(lambda bits: bits > thr)).astype(F32)
        carry_ref[...] = jnp.zeros(carry_ref.shape, F32)

    a = aff_ref[pl.ds(t * SEL_T, SEL_T), :]
    bits = pltpu.bitcast(a, I32)
    thr = thr_ref[...]
    gt = bits > thr
    eq = bits == thr
    r = lax.broadcasted_iota(I32, (SEL_T, SEL_T), 0)
    c = lax.broadcasted_iota(I32, (SEL_T, SEL_T), 1)
    tri = jnp.where(r >= c, 1.0, 0.0).astype(BF16)
    eq_f = jnp.where(eq, 1.0, 0.0)
    cum_eq = jnp.dot(tri, eq_f.astype(BF16), preferred_element_type=F32) + carry_ref[0:1, :]
    sel = jnp.logical_or(gt, jnp.logical_and(eq, cum_eq <= need_ref[...]))
    sel_f = jnp.where(sel, 1.0, 0.0)
    pos = jnp.dot(tri, sel_f.astype(BF16), preferred_element_type=F32) + carry_ref[1:2, :]
    off_ref[0] = carry_ref[1:2, :].astype(I32)
    carry_ref[0:1, :] = carry_ref[0:1, :] + jnp.sum(eq_f, axis=0, keepdims=True)
    carry_ref[1:2, :] = carry_ref[1:2, :] + jnp.sum(sel_f, axis=0, keepdims=True)
    slot_ref[...] = jnp.where(sel, pos - 1.0, -1.0).astype(I32)
    er = lax.broadcasted_iota(I32, (N_EXPERTS, N_EXPERTS), 0)
    ec = lax.broadcasted_iota(I32, (N_EXPERTS, N_EXPERTS), 1)
    before = jnp.where(er < ec, 1.0, 0.0).astype(BF16)
    rank = jnp.dot(sel_f.astype(BF16), before, preferred_element_type=F32).astype(I32)
    tok = t * SEL_T + lax.broadcasted_iota(I32, (SEL_T, N_EXPERTS), 0)
    tokrank_ref[...] = tok * (1 << RANK_BITS) + rank
    tokcnt_ref[...] = jnp.broadcast_to(jnp.sum(sel_f, axis=1, keepdims=True).astype(I32), (SEL_T, N_EXPERTS))


def _lists_kernel(off_ref, slot_ref, tokrank_ref, aff_ref, idx_ref, gate_ref):
    t = pl.program_id(0)

    @pl.when(t == 0)
    def _():
        idx_ref[...] = jnp.zeros(idx_ref.shape, I32)
        gate_ref[...] = jnp.zeros(gate_ref.shape, F32)

    slot = slot_ref[...]
    tokrank = tokrank_ref[...]
    a = aff_ref[...]
    for e in range(N_EXPERTS):
        base = pl.multiple_of((off_ref[t * N_EXPERTS + e] // LANE) * LANE, LANE)
        lane = base + lax.broadcasted_iota(I32, (1, 2 * LANE), 1)
        match = slot[:, e:e + 1] == lane
        win = (slice(e, e + 1), pl.ds(base, 2 * LANE))
        idx_ref[win] += jnp.sum(jnp.where(match, tokrank[:, e:e + 1], 0), axis=0, keepdims=True)
        gate_ref[win] += jnp.sum(jnp.where(match, a[:, e:e + 1], 0.0), axis=0, keepdims=True)


def _select(aff, cap):
    n_tok = aff.shape[0]
    nt = n_tok // SEL_T
    tile = pl.BlockSpec((SEL_T, N_EXPERTS), lambda t: (t, 0))
    tile_shape = jax.ShapeDtypeStruct((n_tok, N_EXPERTS), I32)
    slot, tokrank, tokcnt, offs = pl.pallas_call(
        functools.partial(_slots_kernel, cap=cap, n_tok=n_tok),
        grid=(nt,),
        in_specs=[pl.BlockSpec((n_tok, N_EXPERTS), lambda t: (0, 0))],
        out_specs=[tile, tile, tile, pl.BlockSpec((1, 1, N_EXPERTS), lambda t: (t, 0, 0))],
        out_shape=[tile_shape, tile_shape, tile_shape, jax.ShapeDtypeStruct((nt, 1, N_EXPERTS), I32)],
        scratch_shapes=[pltpu.VMEM((1, N_EXPERTS), I32), pltpu.VMEM((1, N_EXPERTS), F32),
                        pltpu.VMEM((2, N_EXPERTS), F32)],
        compiler_params=_params(("arbitrary",), 32),
        name="expert_slots",
    )(aff)
    offs = offs.reshape(nt, N_EXPERTS)
    width = cap + 2 * LANE
    tile_p = pl.BlockSpec((SEL_T, N_EXPERTS), lambda t, o: (t, 0))
    idx, gates = pl.pallas_call(
        _lists_kernel,
        grid_spec=pltpu.PrefetchScalarGridSpec(
            num_scalar_prefetch=1, grid=(nt,),
            in_specs=[tile_p, tile_p, tile_p],
            out_specs=[pl.BlockSpec((N_EXPERTS, width), lambda t, o: (0, 0)),
                       pl.BlockSpec((N_EXPERTS, width), lambda t, o: (0, 0))]),
        out_shape=[jax.ShapeDtypeStruct((N_EXPERTS, width), I32),
                   jax.ShapeDtypeStruct((N_EXPERTS, width), F32)],
        compiler_params=_params(("arbitrary",), 32),
        name="expert_lists",
    )(offs.reshape(-1), slot, tokrank, aff)
    return idx[:, :cap], gates[:, :cap], offs, tokcnt


def _gather_kernel(idx_ref, hp_ref, xe_ref, sem, *, rows):
    def row_copy(s):
        tok = idx_ref[0, 0, s] >> RANK_BITS
        return pltpu.make_async_copy(hp_ref.at[pl.ds(tok, 1)], xe_ref.at[0, pl.ds(s, 1)], sem)

    def start(s, carry):
        row_copy(s).start()
        return carry

    def wait(s, carry):
        row_copy(s).wait()
        return carry

    lax.fori_loop(0, rows, start, 0, unroll=GATHER_UNROLL)
    lax.fori_loop(0, rows, wait, 0, unroll=GATHER_UNROLL)


GATHER_UNROLL = 8


def _gather(idx, hp, cap, rows):
    half = hp.shape[1]
    n_blocks = cap // rows
    idx3 = idx.reshape(N_EXPERTS * n_blocks, 1, rows)
    return pl.pallas_call(
        functools.partial(_gather_kernel, rows=rows),
        grid=(N_EXPERTS, n_blocks),
        in_specs=[pl.BlockSpec((1, 1, rows), lambda e, r: (e * n_blocks + r, 0, 0), memory_space=pltpu.SMEM),
                  pl.BlockSpec(memory_space=pl.ANY)],
        out_specs=pl.BlockSpec((1, rows, half), lambda e, r: (e, r, 0)),
        out_shape=jax.ShapeDtypeStruct((N_EXPERTS, cap, half), U32),
        scratch_shapes=[pltpu.SemaphoreType.DMA(())],
        compiler_params=_params(("arbitrary", "arbitrary"), 32),
        name="expert_gather",
    )(idx3, hp)


def _unpack(xp):
    lo = pltpu.bitcast(xp << 16, F32).astype(BF16)
    hi = pltpu.bitcast(xp & jnp.uint32(0xFFFF0000), F32).astype(BF16)
    return lo, hi


def _ffn_up_kernel(xe_hbm, wg_ref, wu_ref, h_ref, xe_vmem, wg_bf, wu_bf, sem, *, tr):
    e, j = pl.program_id(0), pl.program_id(1)

    @pl.when(j == 0)
    def _():
        copy = pltpu.make_async_copy(xe_hbm.at[e], xe_vmem, sem)
        copy.start()
        copy.wait()

    wg_bf[...] = wg_ref[0].astype(BF16)
    wu_bf[...] = wu_ref[0].astype(BF16)
    half = xe_vmem.shape[1]

    def row_tile(r, carry):
        rows = pl.ds(pl.multiple_of(r * tr, tr), tr)
        lo, hi = _unpack(xe_vmem[rows, :])

        def proj(w_bf):
            return (jnp.dot(lo, w_bf[:half, :], preferred_element_type=F32)
                    + jnp.dot(hi, w_bf[half:, :], preferred_element_type=F32))

        gate = proj(wg_bf)
        up = proj(wu_bf)
        h_ref[0, rows, :] = (gate * jax.nn.sigmoid(gate) * up).astype(BF16)
        return carry

    lax.fori_loop(0, xe_vmem.shape[0] // tr, row_tile, 0)


def _ffn_up(xe, w_gate, w_up, tr, tf):
    _, cap, half = xe.shape
    d, f = w_gate.shape[1], w_gate.shape[2]
    return pl.pallas_call(
        functools.partial(_ffn_up_kernel, tr=tr),
        grid=(N_EXPERTS, f // tf),
        in_specs=[pl.BlockSpec(memory_space=pl.ANY),
                  pl.BlockSpec((1, d, tf), lambda e, j: (e, 0, j)),
                  pl.BlockSpec((1, d, tf), lambda e, j: (e, 0, j))],
        out_specs=pl.BlockSpec((1, cap, tf), lambda e, j: (e, 0, j)),
        out_shape=jax.ShapeDtypeStruct((N_EXPERTS, cap, f), BF16),
        scratch_shapes=[pltpu.VMEM((cap, half), U32), pltpu.VMEM((d, tf), BF16), pltpu.VMEM((d, tf), BF16),
                        pltpu.SemaphoreType.DMA(())],
        compiler_params=_params(("arbitrary", "arbitrary"), 56),
        name="expert_ffn_up",
    )(xe, w_gate, w_up)


def _pack_halves(x):
    w = x.shape[1] // 2
    bits = pltpu.bitcast(x.astype(BF16).astype(F32), U32)
    return (bits[:, :w] >> 16) | (bits[:, w:] & jnp.uint32(0xFFFF0000))


def _ffn_down_kernel(h_ref, wd_ref, g_ref, o_ref, wd_bf, *, tr):
    wd_bf[...] = wd_ref[0].astype(BF16)

    def row_tile(r, carry):
        rows = pl.ds(pl.multiple_of(r * tr, tr), tr)
        out = jnp.dot(h_ref[0, rows, :], wd_bf[...], preferred_element_type=F32)
        o_ref[0, rows, :] = _pack_halves(g_ref[0, rows, :] * out)
        return carry

    lax.fori_loop(0, h_ref.shape[1] // tr, row_tile, 0)


def _ffn_down(hid, w_down, gates, tr, tn):
    _, cap, f = hid.shape
    d = w_down.shape[2]
    return pl.pallas_call(
        functools.partial(_ffn_down_kernel, tr=tr),
        grid=(N_EXPERTS, d // tn),
        in_specs=[pl.BlockSpec((1, cap, f), lambda e, j: (e, 0, 0)),
                  pl.BlockSpec((1, f, tn), lambda e, j: (e, 0, j)),
                  pl.BlockSpec((1, cap, 1), lambda e, j: (e, 0, 0))],
        out_specs=pl.BlockSpec((1, cap, tn // 2), lambda e, j: (e, 0, j)),
        out_shape=jax.ShapeDtypeStruct((N_EXPERTS, cap, d // 2), U32),
        scratch_shapes=[pltpu.VMEM((f, tn), BF16)],
        compiler_params=_params(("parallel", "arbitrary"), 56),
        name="expert_ffn_down",
    )(hid, w_down, gates.reshape(N_EXPERTS, cap, 1))


def _combine_kernel(lo_ref, hi_ref, maxc_ref, idx_ref, x_ref, cnt_ref, con_ref, y_ref, stage_ref, sem,
                    *, nt, cap, tn):
    t = pl.program_id(0)

    @pl.when(t == 0)
    def _():
        stage_ref[...] = jnp.zeros(stage_ref.shape, U32)

    def row_copy(row):
        v = idx_ref[row]
        tok = (v >> RANK_BITS) - t * SEL_T
        return pltpu.make_async_copy(con_ref.at[pl.ds(row, 1)],
                                     stage_ref.at[v & ((1 << RANK_BITS) - 1), pl.ds(tok, 1)], sem)

    def for_rows(fn):
        def per_row(row, carry):
            fn(row_copy(row))
            return carry

        def per_expert(e, carry):
            k = e * nt + t
            return lax.fori_loop(e * cap + lo_ref[k], e * cap + hi_ref[k], per_row, carry)

        lax.fori_loop(0, N_EXPERTS, per_expert, 0)

    for_rows(lambda copy: copy.start())
    y_ref[...] = x_ref[...]
    for_rows(lambda copy: copy.wait())

    half = tn // 2
    cnt = jnp.broadcast_to(cnt_ref[:, 0:1], (SEL_T, half))

    def add_rank(k, carry):
        live = k < cnt
        for j in range(y_ref.shape[1] // tn):
            xp = jnp.where(live, stage_ref[k, :, j * half:(j + 1) * half], jnp.uint32(0))
            y_ref[:, j * tn:j * tn + half] += pltpu.bitcast(xp << 16, F32)
            y_ref[:, j * tn + half:(j + 1) * tn] += pltpu.bitcast(xp & jnp.uint32(0xFFFF0000), F32)
        return carry

    lax.fori_loop(0, maxc_ref[t], add_rank, 0)


def _combine(x1, idx, contrib, offs, tokcnt, cap, tn):
    n_tok, d = x1.shape
    nt = n_tok // SEL_T
    lo = jnp.transpose(offs)
    hi = jnp.concatenate([lo[:, 1:], jnp.full((N_EXPERTS, 1), cap, I32)], axis=1)
    maxc = jnp.max(tokcnt[:, 0].reshape(nt, SEL_T), axis=1)
    grid_spec = pltpu.PrefetchScalarGridSpec(
        num_scalar_prefetch=4,
        grid=(nt,),
        in_specs=[
            pl.BlockSpec((SEL_T, d), lambda t, *_: (t, 0)),
            pl.BlockSpec((SEL_T, N_EXPERTS), lambda t, *_: (t, 0)),
            pl.BlockSpec(memory_space=pl.ANY),
        ],
        out_specs=pl.BlockSpec((SEL_T, d), lambda t, *_: (t, 0)),
        scratch_shapes=[pltpu.VMEM((N_EXPERTS, SEL_T, d // 2), U32), pltpu.SemaphoreType.DMA(())],
    )
    return pl.pallas_call(
        functools.partial(_combine_kernel, nt=nt, cap=cap, tn=tn),
        grid_spec=grid_spec,
        out_shape=jax.ShapeDtypeStruct((n_tok, d), F32),
        compiler_params=_params(("arbitrary",), 48),
        name="expert_combine",
    )(lo.reshape(-1), hi.reshape(-1), maxc, idx.reshape(-1), x1, tokcnt,
      contrib.reshape(N_EXPERTS * cap, d // 2))


DOWN_TN = 1024
UP_TF = 256
OUT_TN = 1024


def _tiles(n_tok, n_seq, cap):
    return dict(
        tm=min(512, n_seq),
        rows=min(512, cap),
        tr=min(1024, cap),
    )


def _encoder_layer(x, p):
    b, n, d = x.shape
    n_tok = b * n
    cap = EC_FACTOR * n_tok // N_EXPERTS
    tl = _tiles(n_tok, n, cap)
    x2 = x.reshape(n_tok, d)

    proj, v_b = _in_projection(x2, p["norm1"], p["w_in"], p["gains"], p["cos"][n], p["sin"][n], n, tl["tm"])
    oa = _attention_a(proj, b, n, ATT_TQ)
    ob = _attention_b(proj, v_b, p["sink"], p["bias"], b, n)
    x1 = _out_projection(oa, ob, p["out_norm_a"], p["out_norm_b"], p["w_out"], x2, tl["tm"], OUT_TN)

    hp, aff = _router(x1, p["norm2"], p["w_router"], tl["tm"])
    idx, gates, offs, tokcnt = _select(aff, cap)
    xe = _gather(idx, hp, cap, tl["rows"])
    hid = _ffn_up(xe, p["w_gate"], p["w_up"], tl["tr"], UP_TF)
    contrib = _ffn_down(hid, p["w_down"], gates, tl["tr"], DOWN_TN)
    y = _combine(x1, idx, contrib, offs, tokcnt, cap, DOWN_TN)
    return y.reshape(b, n, d)


def kernel(x_prompt, x_sample, norm1, w_in, q_norm_a, k_norm_a, q_norm_b, k_norm_b, sink_b, rel_bias,
           out_norm_a, out_norm_b, w_out, norm2, w_router, w_gate, w_up, w_down):
    depth = norm1.shape[0]
    seqs = sorted({x_prompt.shape[1], x_sample.shape[1]})
    tables = {n: _rope_tables(n) for n in seqs}
    bias = _window_bias(rel_bias)
    ones = jnp.ones((KVW,), F32)

    layers = []
    for l in range(depth):
        gains = jnp.concatenate([jnp.tile(q_norm_a[l], H_A), jnp.tile(k_norm_a[l], KV_A), ones,
                                 jnp.tile(q_norm_b[l], H_B), jnp.tile(k_norm_b[l], KV_B), ones])
        layers.append(dict(
            norm1=norm1[l], w_in=w_in[l].astype(BF16), gains=gains.reshape(1, IN_WIDTH),
            cos={n: tables[n][0] for n in seqs}, sin={n: tables[n][1] for n in seqs},
            sink=sink_b[l], bias=bias,
            out_norm_a=out_norm_a[l], out_norm_b=out_norm_b[l], w_out=w_out[l].astype(BF16),
            norm2=norm2[l], w_router=w_router[l].astype(BF16),
            w_gate=w_gate[l], w_up=w_up[l], w_down=w_down[l],
        ))

    def run(x):
        for p in layers:
            x = _encoder_layer(x, p)
        return x

    return (run(x_prompt), run(x_sample))
```

```python
import functools
import math

import jax
import jax.numpy as jnp
from jax import lax
from jax.experimental import pallas as pl
from jax.experimental.pallas import tpu as pltpu

F32 = jnp.float32
BF16 = jnp.bfloat16
I32 = jnp.int32
U32 = jnp.uint32

HEAD_DIM = 128
H_A = 16
KV_A = 4
H_B = 16
KV_B = 4
GROUP = 4
W_A = H_A * HEAD_DIM
W_B = H_B * HEAD_DIM
KVW = KV_A * HEAD_DIM
IN_WIDTH = W_A + 2 * KVW + W_B + 2 * KVW
GRID_W = 64
HALF_ROT = HEAD_DIM // 2
ROPE_THETA = 10000.0
REL_BUCKETS = 32
REL_MAX_DIST = 128
WINDOW = 128
N_EXPERTS = 16
EC_FACTOR = 2
NORM_EPS = 1e-6
NEG_INF = -1e30
SCALE = 1.0 / math.sqrt(HEAD_DIM)

V7X_VMEM_BYTES = 64 * 1024 * 1024
LANE = 128


def _vmem(mib):
    assert mib * 1024 * 1024 < V7X_VMEM_BYTES
    return mib * 1024 * 1024


def _params(sem, mib):
    return pltpu.CompilerParams(dimension_semantics=sem, vmem_limit_bytes=_vmem(mib))


IN_TN = 512
_ROPE_TILES = 5
_VA_TILE = 5
_N_TILES = IN_WIDTH // IN_TN
_MAIN_TILES = _N_TILES - 1


def _inproj_kernel(x_ref, g1_ref, w_ref, gain_ref, cos_ref, sin_ref, o_ref, vb_ref, h_ref, raw_ref):
    j = pl.program_id(1)

    @pl.when(j == 0)
    def _():
        x = x_ref[...]
        ms = jnp.mean(x * x, axis=-1, keepdims=True)
        h_ref[...] = (x * lax.rsqrt(ms + NORM_EPS) * g1_ref[...]).astype(BF16)
        raw_ref[...] = jnp.zeros(raw_ref.shape, F32)

    prev = raw_ref[...]
    acc = jnp.dot(h_ref[...], w_ref[...], preferred_element_type=F32)
    jp = j - 1
    is_v = jp == _VA_TILE
    is_rope = jp < _ROPE_TILES
    is_q = jnp.logical_or(jp < W_A // IN_TN,
                          jnp.logical_and(jp > _VA_TILE, jp < (IN_WIDTH - 2 * KVW) // IN_TN))
    outs = []
    for hh in range(IN_TN // HEAD_DIM):
        a = prev[:, hh * HEAD_DIM:(hh + 1) * HEAD_DIM]
        ms = jnp.mean(a * a, axis=-1, keepdims=True)
        r = jnp.where(is_v, 1.0, lax.rsqrt(ms + NORM_EPS))
        outs.append(a * r * gain_ref[:, hh * HEAD_DIM:(hh + 1) * HEAD_DIM])
    y = jnp.concatenate(outs, axis=-1)
    reps = IN_TN // HEAD_DIM
    c = jnp.concatenate([jnp.where(is_rope, cos_ref[...], 1.0)] * reps, axis=-1)
    s = jnp.concatenate([jnp.where(is_rope, sin_ref[...], 0.0)] * reps, axis=-1)
    lane = lax.broadcasted_iota(I32, y.shape, 1)
    first_half = (lane % (2 * (HALF_ROT // 2))) < (HALF_ROT // 2)
    partner = jnp.where(first_half,
                        pltpu.roll(y, IN_TN - HALF_ROT // 2, axis=1),
                        pltpu.roll(y, HALF_ROT // 2, axis=1))
    o_ref[...] = ((y * c + partner * s) * jnp.where(is_q, SCALE * LOG2E, 1.0)).astype(BF16)
    raw_ref[...] = acc

    @pl.when(j == _N_TILES - 1)
    def _():
        vb_ref[...] = acc.astype(BF16)


def _rope_tables(n):
    rows = n // GRID_W
    row = jnp.repeat(jnp.arange(rows, dtype=F32), GRID_W)
    col = jnp.tile(jnp.arange(GRID_W, dtype=F32), rows)
    inv = ROPE_THETA ** (-jnp.arange(0, HALF_ROT, 2, dtype=F32) / HALF_ROT)
    ar, ac = row[:, None] * inv, col[:, None] * inv
    cr, sr, cc, sc = jnp.cos(ar), jnp.sin(ar), jnp.cos(ac), jnp.sin(ac)
    cos_t = jnp.concatenate([cr, cr, cc, cc], axis=-1)
    sin_t = jnp.concatenate([-sr, sr, -sc, sc], axis=-1)
    return cos_t, sin_t


def _in_projection(x2, norm1, w_in_bf, gains, cos_t, sin_t, n_seq, tm):
    n_tok, d = x2.shape
    seq_blocks = n_seq // tm
    return pl.pallas_call(
        _inproj_kernel,
        grid=(n_tok // tm, _N_TILES),
        in_specs=[
            pl.BlockSpec((tm, d), lambda i, j: (i, 0)),
            pl.BlockSpec((1, d), lambda i, j: (0, 0)),
            pl.BlockSpec((d, IN_TN), lambda i, j: (0, j)),
            pl.BlockSpec((1, IN_TN), lambda i, j: (0, jnp.maximum(j - 1, 0))),
            pl.BlockSpec((tm, HEAD_DIM), lambda i, j: (i % seq_blocks, 0)),
            pl.BlockSpec((tm, HEAD_DIM), lambda i, j: (i % seq_blocks, 0)),
        ],
        out_specs=[pl.BlockSpec((tm, IN_TN), lambda i, j: (i, jnp.maximum(j - 1, 0))),
                   pl.BlockSpec((tm, IN_TN), lambda i, j: (i, 0))],
        out_shape=[jax.ShapeDtypeStruct((n_tok, _MAIN_TILES * IN_TN), BF16),
                   jax.ShapeDtypeStruct((n_tok, IN_TN), BF16)],
        scratch_shapes=[pltpu.VMEM((tm, d), BF16), pltpu.VMEM((tm, IN_TN), F32)],
        compiler_params=_params(("parallel", "arbitrary"), 48),
        name="in_projection",
    )(x2, norm1.reshape(1, d), w_in_bf, gains, cos_t, sin_t)


ATT_TQ = 128
ATT_TKC = 256
ATT_UNROLL = 32
LOG2E = 1.4426950408889634


ATT_UNROLL_BOUNDED = 64
ATT_PV_GROUP = 16
ATT_BOUND_LIMIT = 40.0
ATT_BOUND_SLACK = 1.001


def _flash_kernel(q_ref, k_ref, v_ref, o_ref, vt_ref, kmax_ref, *, n_chunks, unroll):
    @pl.when(pl.program_id(2) == 0)
    def _():
        def prepare_chunk(c, kmax2):
            st = pl.multiple_of(c * ATT_TKC, ATT_TKC)
            vt_ref[:, pl.ds(st, ATT_TKC)] = v_ref[pl.ds(st, ATT_TKC), :].T
            kf = k_ref[pl.ds(st, ATT_TKC), :].astype(F32)
            return jnp.maximum(kmax2, jnp.max(jnp.sum(kf * kf, axis=1, keepdims=True), axis=0, keepdims=True))
        kmax2 = lax.fori_loop(0, n_chunks, prepare_chunk, jnp.zeros((1, 1), F32))
        kmax_ref[...] = jnp.broadcast_to(jnp.sqrt(kmax2), kmax_ref.shape)

    q = q_ref[...]
    tq = q.shape[0]
    q_t = jnp.concatenate([q[:, g * HEAD_DIM:(g + 1) * HEAD_DIM].T for g in range(GROUP)], axis=1)
    nq = GROUP * tq
    qf = q_t.astype(F32)
    bound = jnp.sqrt(jnp.sum(qf * qf, axis=0, keepdims=True)) * kmax_ref[...] * ATT_BOUND_SLACK

    def write(acc, l):
        o_t = acc / l
        for g in range(GROUP):
            o_ref[:, g * HEAD_DIM:(g + 1) * HEAD_DIM] = o_t[:, g * tq:(g + 1) * tq].T

    def start(c, u):
        return pl.multiple_of((c * unroll + u) * ATT_TKC, ATT_TKC)

    bounded = jnp.max(bound) <= ATT_BOUND_LIMIT

    @pl.when(bounded)
    def _():
        trip = math.gcd(n_chunks, ATT_UNROLL_BOUNDED)
        group = math.gcd(trip, ATT_PV_GROUP)

        def body(c, carry):
            l, acc = carry
            for g0 in range(0, trip, group):
                first = (c * trip + g0) * ATT_TKC
                probs = []
                for u in range(group):
                    keys = pl.ds(pl.multiple_of(first + u * ATT_TKC, ATT_TKC), ATT_TKC)
                    s = jnp.dot(k_ref[keys, :], q_t, preferred_element_type=F32)
                    p = jnp.exp2(s - bound)
                    l = l + jnp.sum(p, axis=0, keepdims=True)
                    probs.append(p.astype(BF16))
                keys = pl.ds(pl.multiple_of(first, group * ATT_TKC), group * ATT_TKC)
                acc = acc + jnp.dot(vt_ref[:, keys], jnp.concatenate(probs, axis=0), preferred_element_type=F32)
            return l, acc

        init = (jnp.zeros((1, nq), F32), jnp.zeros((HEAD_DIM, nq), F32))
        l, acc = lax.fori_loop(0, n_chunks // trip, body, init)
        write(acc, l)

    @pl.when(jnp.logical_not(bounded))
    def _():
        def body(c, carry):
            m, l, acc = carry
            scores = [jnp.dot(k_ref[pl.ds(start(c, u), ATT_TKC), :], q_t, preferred_element_type=F32)
                      for u in range(unroll)]
            for u, s in enumerate(scores):
                m_new = jnp.maximum(m, jnp.max(s, axis=0, keepdims=True))
                alpha = jnp.exp2(m - m_new)
                p = jnp.exp2(s - m_new)
                l = alpha * l + jnp.sum(p, axis=0, keepdims=True)
                pv = jnp.dot(vt_ref[:, pl.ds(start(c, u), ATT_TKC)], p.astype(BF16), preferred_element_type=F32)
                acc = alpha * acc + pv
                m = m_new
            return m, l, acc

        init = (jnp.full((1, nq), -jnp.inf, F32), jnp.zeros((1, nq), F32), jnp.zeros((HEAD_DIM, nq), F32))
        _, l, acc = lax.fori_loop(0, n_chunks // unroll, body, init)
        write(acc, l)


def _attention_a(proj, b, n, tq):
    n_tok = b * n
    qb = n // tq
    k_col = W_A // HEAD_DIM
    v_col = (W_A + KVW) // HEAD_DIM
    n_chunks = n // ATT_TKC
    unroll = math.gcd(n_chunks, ATT_UNROLL)
    return pl.pallas_call(
        functools.partial(_flash_kernel, n_chunks=n_chunks, unroll=unroll),
        grid=(b, KV_A, qb),
        in_specs=[
            pl.BlockSpec((tq, GROUP * HEAD_DIM), lambda bi, h, qi: (bi * qb + qi, h)),
            pl.BlockSpec((n, HEAD_DIM), lambda bi, h, qi: (bi, k_col + h)),
            pl.BlockSpec((n, HEAD_DIM), lambda bi, h, qi: (bi, v_col + h)),
        ],
        out_specs=pl.BlockSpec((tq, GROUP * HEAD_DIM), lambda bi, h, qi: (bi * qb + qi, h)),
        out_shape=jax.ShapeDtypeStruct((n_tok, W_A), F32),
        scratch_shapes=[pltpu.VMEM((HEAD_DIM, n), BF16), pltpu.VMEM((1, GROUP * tq), F32)],
        compiler_params=_params(("arbitrary", "arbitrary", "arbitrary"), 48),
        name="attention_global",
    )(proj, proj, proj)


WIN_TQ = 2 * WINDOW
WIN_TK = WIN_TQ + 2 * WINDOW


def _t5_buckets(rel):
    nb = REL_BUCKETS // 2
    max_exact = nb // 2
    ret = jnp.where(rel > 0, nb, 0)
    n = jnp.abs(rel)
    nf = jnp.maximum(n, 1).astype(F32)
    large = max_exact + (jnp.log(nf / max_exact) / math.log(REL_MAX_DIST / max_exact)
                         * (nb - max_exact)).astype(I32)
    large = jnp.minimum(large, nb - 1)
    return ret + jnp.where(n < max_exact, n, large)


def _window_bias(rel_bias):
    span = WIN_TQ + WIN_TK
    kk = jnp.arange(span)
    rel = jnp.where(kk < WIN_TK, kk, kk - span) - WINDOW
    tbl = jnp.where((jnp.abs(rel) <= WINDOW)[:, None], rel_bias[_t5_buckets(rel)].astype(F32), NEG_INF)
    flat = jnp.tile(tbl.T, (1, WIN_TQ))[:, :WIN_TQ * (span - 1)]
    bias = flat.reshape(H_B, WIN_TQ, span - 1)[:, :, :WIN_TK]
    j = jnp.arange(WIN_TK)
    before, after = j < WINDOW, j >= WINDOW + WIN_TQ
    masks = jnp.stack([jnp.zeros_like(before), before, after, jnp.logical_or(before, after)])
    bias = jnp.where(masks[:, None, None, :], NEG_INF, bias[None])
    bias = bias.reshape(4, KV_B, GROUP, WIN_TQ, WIN_TK).transpose(0, 1, 4, 2, 3)
    return bias.reshape(4, KV_B, WIN_TK, GROUP * WIN_TQ) * LOG2E


def _window_kernel(sink_ref, q_ref, kp_ref, kc_ref, kn_ref, vp_ref, vc_ref, vn_ref, bias_ref, o_ref):
    h = pl.program_id(0)
    q = q_ref[...]
    q_all = jnp.concatenate([q[:, g * HEAD_DIM:(g + 1) * HEAD_DIM] for g in range(GROUP)], axis=0)
    kcat = jnp.concatenate([kp_ref[...], kc_ref[...], kn_ref[...]], axis=0)
    vcat = jnp.concatenate([vp_ref[...], vc_ref[...], vn_ref[...]], axis=0)
    s = lax.dot_general(kcat, q_all, (((1,), (1,)), ((), ())), preferred_element_type=F32) + bias_ref[0, 0]
    head = lax.broadcasted_iota(I32, (1, GROUP * WIN_TQ), 1) // WIN_TQ
    snk = jnp.zeros((1, GROUP * WIN_TQ), F32)
    for g in range(GROUP):
        snk = jnp.where(head == g, sink_ref[h * GROUP + g] * LOG2E, snk)
    m = jnp.maximum(jnp.max(s, axis=0, keepdims=True), snk)
    e = jnp.exp2(s - m)
    denom = jnp.sum(e, axis=0, keepdims=True) + jnp.exp2(snk - m)
    o_t = lax.dot_general(vcat, e.astype(BF16), (((0,), (0,)), ((), ())), preferred_element_type=F32) / denom
    for g in range(GROUP):
        o_ref[:, g * HEAD_DIM:(g + 1) * HEAD_DIM] = o_t[:, g * WIN_TQ:(g + 1) * WIN_TQ].T


def _attention_b(proj, v_b, sink, bias, b, n):
    n_tok = b * n
    qb = n // WIN_TQ
    nb128 = n // WINDOW
    base = W_A + 2 * KVW
    q_col = base // (GROUP * HEAD_DIM)
    k_col = (base + W_B) // HEAD_DIM
    v_col = 0

    def prev_map(col):
        return lambda h, bi, i, s: (bi * nb128 + jnp.maximum(2 * i - 1, 0), col + h)

    def cur_map(col):
        return lambda h, bi, i, s: (bi * qb + i, col + h)

    def next_map(col):
        return lambda h, bi, i, s: (bi * nb128 + jnp.minimum(2 * i + 2, nb128 - 1), col + h)

    grid_spec = pltpu.PrefetchScalarGridSpec(
        num_scalar_prefetch=1,
        grid=(KV_B, b, qb),
        in_specs=[
            pl.BlockSpec((WIN_TQ, GROUP * HEAD_DIM), lambda h, bi, i, s: (bi * qb + i, q_col + h)),
            pl.BlockSpec((WINDOW, HEAD_DIM), prev_map(k_col)),
            pl.BlockSpec((WIN_TQ, HEAD_DIM), cur_map(k_col)),
            pl.BlockSpec((WINDOW, HEAD_DIM), next_map(k_col)),
            pl.BlockSpec((WINDOW, HEAD_DIM), prev_map(v_col)),
            pl.BlockSpec((WIN_TQ, HEAD_DIM), cur_map(v_col)),
            pl.BlockSpec((WINDOW, HEAD_DIM), next_map(v_col)),
            pl.BlockSpec((1, 1, WIN_TK, GROUP * WIN_TQ),
                         lambda h, bi, i, s: (jnp.where(i == 0, 1, 0) + jnp.where(i == qb - 1, 2, 0), h, 0, 0)),
        ],
        out_specs=pl.BlockSpec((WIN_TQ, GROUP * HEAD_DIM), lambda h, bi, i, s: (bi * qb + i, h)),
    )
    return pl.pallas_call(
        _window_kernel,
        grid_spec=grid_spec,
        out_shape=jax.ShapeDtypeStruct((n_tok, W_B), F32),
        compiler_params=_params(("parallel", "parallel", "parallel"), 32),
        name="attention_window",
    )(sink, proj, proj, proj, proj, v_b, v_b, v_b, bias)


def _outproj_kernel(oa_ref, ob_ref, ga_ref, gb_ref, w_ref, x_ref, o_ref, mix_ref):
    j = pl.program_id(1)

    @pl.when(j == 0)
    def _():
        for src, gain, lo in ((oa_ref, ga_ref, 0), (ob_ref, gb_ref, W_A)):
            o = src[...]
            ms = jnp.mean(o * o, axis=-1, keepdims=True)
            mix_ref[:, lo:lo + o.shape[1]] = (o * lax.rsqrt(ms + NORM_EPS) * gain[...]).astype(BF16)

    o_ref[...] = x_ref[...] + jnp.dot(mix_ref[...], w_ref[...], preferred_element_type=F32)


def _out_projection(oa, ob, gain_a, gain_b, w_out_bf, x2, tm, tn):
    n_tok, d = x2.shape
    return pl.pallas_call(
        _outproj_kernel,
        grid=(n_tok // tm, d // tn),
        in_specs=[
            pl.BlockSpec((tm, W_A), lambda i, j: (i, 0)),
            pl.BlockSpec((tm, W_B), lambda i, j: (i, 0)),
            pl.BlockSpec((1, W_A), lambda i, j: (0, 0)),
            pl.BlockSpec((1, W_B), lambda i, j: (0, 0)),
            pl.BlockSpec((W_A + W_B, tn), lambda i, j: (0, j)),
            pl.BlockSpec((tm, tn), lambda i, j: (i, j)),
        ],
        out_specs=pl.BlockSpec((tm, tn), lambda i, j: (i, j)),
        out_shape=jax.ShapeDtypeStruct((n_tok, d), F32),
        scratch_shapes=[pltpu.VMEM((tm, W_A + W_B), BF16)],
        compiler_params=_params(("parallel", "arbitrary"), 48),
        name="out_projection",
    )(oa, ob, gain_a.reshape(1, W_A), gain_b.reshape(1, W_B), w_out_bf, x2)


def _router_kernel(x_ref, g_ref, wr_ref, hp_ref, aff_ref):
    x = x_ref[...]
    ms = jnp.mean(x * x, axis=-1, keepdims=True)
    hb = (x * lax.rsqrt(ms + NORM_EPS) * g_ref[...]).astype(BF16)
    logits = jnp.dot(hb, wr_ref[...], preferred_element_type=F32)
    e = jnp.exp(logits - jnp.max(logits, axis=-1, keepdims=True))
    aff_ref[...] = e / jnp.sum(e, axis=-1, keepdims=True)
    half = hb.shape[1] // 2
    bits = pltpu.bitcast(hb.astype(F32), U32)
    hp_ref[...] = (bits[:, :half] >> 16) | (bits[:, half:] & jnp.uint32(0xFFFF0000))


def _router(x1, norm2, w_router_bf, tm):
    n_tok, d = x1.shape
    return pl.pallas_call(
        _router_kernel,
        grid=(n_tok // tm,),
        in_specs=[
            pl.BlockSpec((tm, d), lambda i: (i, 0)),
            pl.BlockSpec((1, d), lambda i: (0, 0)),
            pl.BlockSpec((d, N_EXPERTS), lambda i: (0, 0)),
        ],
        out_specs=[pl.BlockSpec((tm, d // 2), lambda i: (i, 0)),
                   pl.BlockSpec((tm, N_EXPERTS), lambda i: (i, 0))],
        out_shape=[jax.ShapeDtypeStruct((n_tok, d // 2), U32),
                   jax.ShapeDtypeStruct((n_tok, N_EXPERTS), F32)],
        compiler_params=_params(("parallel",), 48),
        name="router",
    )(x1, norm2.reshape(1, d), w_router_bf)


SEL_T = 128
SEL_CHUNK = 1024
RANK_BITS = 4
assert N_EXPERTS <= 1 << RANK_BITS


def _slots_kernel(aff_ref, slot_ref, tokrank_ref, tokcnt_ref, off_ref, thr_ref, need_ref, carry_ref, *, cap, n_tok):
    t = pl.program_id(0)
    n_chunks = n_tok // SEL_CHUNK

    def count(pred_fn):
        def body(c, tot):
            bits = pltpu.bitcast(aff_ref[pl.ds(c * SEL_CHUNK, SEL_CHUNK), :], I32)
            return tot + jnp.sum(pred_fn(bits).astype(I32), axis=0, keepdims=True)
        return lax.fori_loop(0, n_chunks, body, jnp.zeros((1, N_EXPERTS), I32))

    @pl.when(t == 0)
    def _():
        def bit_body(b, thr):
            cand = thr | jnp.left_shift(jnp.int32(1), 30 - b)
            return jnp.where(count(lambda bits: bits >= cand) >= cap, cand, thr)
        thr = lax.fori_loop(0, 31, bit_body, jnp.zeros((1, N_EXPERTS), I32))
        thr_ref[...] = thr
        need_ref[...] = (cap - count(lambda bits: bits > thr)).astype(F32)
        carry_ref[...] = jnp.zeros(carry_ref.shape, F32)

    a = aff_ref[pl.ds(t * SEL_T, SEL_T), :]
    bits = pltpu.bitcast(a, I32)
    thr = thr_ref[...]
    gt = bits > thr
    eq = bits == thr
    r = lax.broadcasted_iota(I32, (SEL_T, SEL_T), 0)
    c = lax.broadcasted_iota(I32, (SEL_T, SEL_T), 1)
    tri = jnp.where(r >= c, 1.0, 0.0).astype(BF16)
    eq_f = jnp.where(eq, 1.0, 0.0)
    cum_eq = jnp.dot(tri, eq_f.astype(BF16), preferred_element_type=F32) + carry_ref[0:1, :]
    sel = jnp.logical_or(gt, jnp.logical_and(eq, cum_eq <= need_ref[...]))
    sel_f = jnp.where(sel, 1.0, 0.0)
    pos = jnp.dot(tri, sel_f.astype(BF16), preferred_element_type=F32) + carry_ref[1:2, :]
    off_ref[0] = carry_ref[1:2, :].astype(I32)
    carry_ref[0:1, :] = carry_ref[0:1, :] + jnp.sum(eq_f, axis=0, keepdims=True)
    carry_ref[1:2, :] = carry_ref[1:2, :] + jnp.sum(sel_f, axis=0, keepdims=True)
    slot_ref[...] = jnp.where(sel, pos - 1.0, -1.0).astype(I32)
    er = lax.broadcasted_iota(I32, (N_EXPERTS, N_EXPERTS), 0)
    ec = lax.broadcasted_iota(I32, (N_EXPERTS, N_EXPERTS), 1)
    before = jnp.where(er < ec, 1.0, 0.0).astype(BF16)
    rank = jnp.dot(sel_f.astype(BF16), before, preferred_element_type=F32).astype(I32)
    tok = t * SEL_T + lax.broadcasted_iota(I32, (SEL_T, N_EXPERTS), 0)
    tokrank_ref[...] = tok * (1 << RANK_BITS) + rank
    tokcnt_ref[...] = jnp.broadcast_to(jnp.sum(sel_f, axis=1, keepdims=True).astype(I32), (SEL_T, N_EXPERTS))


def _lists_kernel(off_ref, slot_ref, tokrank_ref, aff_ref, idx_ref, gate_ref):
    t = pl.program_id(0)

    @pl.when(t == 0)
    def _():
        idx_ref[...] = jnp.zeros(idx_ref.shape, I32)
        gate_ref[...] = jnp.zeros(gate_ref.shape, F32)

    slot = slot_ref[...]
    tokrank = tokrank_ref[...]
    a = aff_ref[...]
    for e in range(N_EXPERTS):
        base = pl.multiple_of((off_ref[t * N_EXPERTS + e] // LANE) * LANE, LANE)
        lane = base + lax.broadcasted_iota(I32, (1, 2 * LANE), 1)
        match = slot[:, e:e + 1] == lane
        win = (slice(e, e + 1), pl.ds(base, 2 * LANE))
        idx_ref[win] += jnp.sum(jnp.where(match, tokrank[:, e:e + 1], 0), axis=0, keepdims=True)
        gate_ref[win] += jnp.sum(jnp.where(match, a[:, e:e + 1], 0.0), axis=0, keepdims=True)


def _select(aff, cap):
    n_tok = aff.shape[0]
    nt = n_tok // SEL_T
    tile = pl.BlockSpec((SEL_T, N_EXPERTS), lambda t: (t, 0))
    tile_shape = jax.ShapeDtypeStruct((n_tok, N_EXPERTS), I32)
    slot, tokrank, tokcnt, offs = pl.pallas_call(
        functools.partial(_slots_kernel, cap=cap, n_tok=n_tok),
        grid=(nt,),
        in_specs=[pl.BlockSpec((n_tok, N_EXPERTS), lambda t: (0, 0))],
        out_specs=[tile, tile, tile, pl.BlockSpec((1, 1, N_EXPERTS), lambda t: (t, 0, 0))],
        out_shape=[tile_shape, tile_shape, tile_shape, jax.ShapeDtypeStruct((nt, 1, N_EXPERTS), I32)],
        scratch_shapes=[pltpu.VMEM((1, N_EXPERTS), I32), pltpu.VMEM((1, N_EXPERTS), F32),
                        pltpu.VMEM((2, N_EXPERTS), F32)],
        compiler_params=_params(("arbitrary",), 32),
        name="expert_slots",
    )(aff)
    offs = offs.reshape(nt, N_EXPERTS)
    width = cap + 2 * LANE
    tile_p = pl.BlockSpec((SEL_T, N_EXPERTS), lambda t, o: (t, 0))
    idx, gates = pl.pallas_call(
        _lists_kernel,
        grid_spec=pltpu.PrefetchScalarGridSpec(
            num_scalar_prefetch=1, grid=(nt,),
            in_specs=[tile_p, tile_p, tile_p],
            out_specs=[pl.BlockSpec((N_EXPERTS, width), lambda t, o: (0, 0)),
                       pl.BlockSpec((N_EXPERTS, width), lambda t, o: (0, 0))]),
        out_shape=[jax.ShapeDtypeStruct((N_EXPERTS, width), I32),
                   jax.ShapeDtypeStruct((N_EXPERTS, width), F32)],
        compiler_params=_params(("arbitrary",), 32),
        name="expert_lists",
    )(offs.reshape(-1), slot, tokrank, aff)
    return idx[:, :cap], gates[:, :cap], offs, tokcnt


def _gather_kernel(idx_ref, hp_ref, xe_ref, sem, *, rows):
    def row_copy(s):
        tok = idx_ref[0, 0, s] >> RANK_BITS
        return pltpu.make_async_copy(hp_ref.at[pl.ds(tok, 1)], xe_ref.at[0, pl.ds(s, 1)], sem)

    def start(s, carry):
        row_copy(s).start()
        return carry

    def wait(s, carry):
        row_copy(s).wait()
        return carry

    lax.fori_loop(0, rows, start, 0, unroll=GATHER_UNROLL)
    lax.fori_loop(0, rows, wait, 0, unroll=GATHER_UNROLL)


GATHER_UNROLL = 8


def _gather(idx, hp, cap, rows):
    half = hp.shape[1]
    n_blocks = cap // rows
    idx3 = idx.reshape(N_EXPERTS * n_blocks, 1, rows)
    return pl.pallas_call(
        functools.partial(_gather_kernel, rows=rows),
        grid=(N_EXPERTS, n_blocks),
        in_specs=[pl.BlockSpec((1, 1, rows), lambda e, r: (e * n_blocks + r, 0, 0), memory_space=pltpu.SMEM),
                  pl.BlockSpec(memory_space=pl.ANY)],
        out_specs=pl.BlockSpec((1, rows, half), lambda e, r: (e, r, 0)),
        out_shape=jax.ShapeDtypeStruct((N_EXPERTS, cap, half), U32),
        scratch_shapes=[pltpu.SemaphoreType.DMA(())],
        compiler_params=_params(("arbitrary", "arbitrary"), 32),
        name="expert_gather",
    )(idx3, hp)


def _unpack(xp):
    lo = pltpu.bitcast(xp << 16, F32).astype(BF16)
    hi = pltpu.bitcast(xp & jnp.uint32(0xFFFF0000), F32).astype(BF16)
    return lo, hi


def _ffn_up_kernel(xe_hbm, wg_ref, wu_ref, h_ref, xe_vmem, wg_bf, wu_bf, sem, *, tr):
    e, j, r = pl.program_id(0), pl.program_id(1), pl.program_id(2)

    @pl.when(jnp.logical_and(j == 0, r == 0))
    def _():
        copy = pltpu.make_async_copy(xe_hbm.at[e], xe_vmem, sem)
        copy.start()
        copy.wait()

    @pl.when(r == 0)
    def _():
        wg_bf[...] = wg_ref[0].astype(BF16)
        wu_bf[...] = wu_ref[0].astype(BF16)

    lo, hi = _unpack(xe_vmem[pl.ds(pl.multiple_of(r * tr, tr), tr), :])
    half = lo.shape[1]

    def proj(w_bf):
        return (jnp.dot(lo, w_bf[:half, :], preferred_element_type=F32)
                + jnp.dot(hi, w_bf[half:, :], preferred_element_type=F32))

    gate = proj(wg_bf)
    up = proj(wu_bf)
    h_ref[0] = (gate * jax.nn.sigmoid(gate) * up).astype(BF16)


def _ffn_up(xe, w_gate, w_up, tr, tf):
    _, cap, half = xe.shape
    d, f = w_gate.shape[1], w_gate.shape[2]
    return pl.pallas_call(
        functools.partial(_ffn_up_kernel, tr=tr),
        grid=(N_EXPERTS, f // tf, cap // tr),
        in_specs=[pl.BlockSpec(memory_space=pl.ANY),
                  pl.BlockSpec((1, d, tf), lambda e, j, r: (e, 0, j)),
                  pl.BlockSpec((1, d, tf), lambda e, j, r: (e, 0, j))],
        out_specs=pl.BlockSpec((1, tr, tf), lambda e, j, r: (e, r, j)),
        out_shape=jax.ShapeDtypeStruct((N_EXPERTS, cap, f), BF16),
        scratch_shapes=[pltpu.VMEM((cap, half), U32), pltpu.VMEM((d, tf), BF16), pltpu.VMEM((d, tf), BF16),
                        pltpu.SemaphoreType.DMA(())],
        compiler_params=_params(("arbitrary", "arbitrary", "arbitrary"), 56),
        name="expert_ffn_up",
    )(xe, w_gate, w_up)


def _pack_halves(x):
    w = x.shape[1] // 2
    bits = pltpu.bitcast(x.astype(BF16).astype(F32), U32)
    return (bits[:, :w] >> 16) | (bits[:, w:] & jnp.uint32(0xFFFF0000))


def _ffn_down_kernel(h_ref, wd_ref, g_ref, o_ref, wd_bf, *, tr):
    wd_bf[...] = wd_ref[0].astype(BF16)

    def row_tile(r, carry):
        rows = pl.ds(pl.multiple_of(r * tr, tr), tr)
        out = jnp.dot(h_ref[0, rows, :], wd_bf[...], preferred_element_type=F32)
        o_ref[0, rows, :] = _pack_halves(g_ref[0, rows, :] * out)
        return carry

    lax.fori_loop(0, h_ref.shape[1] // tr, row_tile, 0)


def _ffn_down(hid, w_down, gates, tr, tn):
    _, cap, f = hid.shape
    d = w_down.shape[2]
    return pl.pallas_call(
        functools.partial(_ffn_down_kernel, tr=tr),
        grid=(N_EXPERTS, d // tn),
        in_specs=[pl.BlockSpec((1, cap, f), lambda e, j: (e, 0, 0)),
                  pl.BlockSpec((1, f, tn), lambda e, j: (e, 0, j)),
                  pl.BlockSpec((1, cap, 1), lambda e, j: (e, 0, 0))],
        out_specs=pl.BlockSpec((1, cap, tn // 2), lambda e, j: (e, 0, j)),
        out_shape=jax.ShapeDtypeStruct((N_EXPERTS, cap, d // 2), U32),
        scratch_shapes=[pltpu.VMEM((f, tn), BF16)],
        compiler_params=_params(("parallel", "arbitrary"), 56),
        name="expert_ffn_down",
    )(hid, w_down, gates.reshape(N_EXPERTS, cap, 1))


def _combine_kernel(lo_ref, hi_ref, maxc_ref, idx_ref, x_ref, cnt_ref, con_ref, y_ref, stage_ref, sem,
                    *, nt, cap, tn):
    t = pl.program_id(0)

    @pl.when(t == 0)
    def _():
        stage_ref[...] = jnp.zeros(stage_ref.shape, U32)

    def row_copy(row):
        v = idx_ref[row]
        tok = (v >> RANK_BITS) - t * SEL_T
        return pltpu.make_async_copy(con_ref.at[pl.ds(row, 1)],
                                     stage_ref.at[v & ((1 << RANK_BITS) - 1), pl.ds(tok, 1)], sem)

    def for_rows(fn):
        def per_row(row, carry):
            fn(row_copy(row))
            return carry

        def per_expert(e, carry):
            k = e * nt + t
            return lax.fori_loop(e * cap + lo_ref[k], e * cap + hi_ref[k], per_row, carry)

        lax.fori_loop(0, N_EXPERTS, per_expert, 0)

    for_rows(lambda copy: copy.start())
    y_ref[...] = x_ref[...]
    for_rows(lambda copy: copy.wait())

    half = tn // 2
    cnt = jnp.broadcast_to(cnt_ref[:, 0:1], (SEL_T, half))

    def add_rank(k, carry):
        live = k < cnt
        for j in range(y_ref.shape[1] // tn):
            xp = jnp.where(live, stage_ref[k, :, j * half:(j + 1) * half], jnp.uint32(0))
            y_ref[:, j * tn:j * tn + half] += pltpu.bitcast(xp << 16, F32)
            y_ref[:, j * tn + half:(j + 1) * tn] += pltpu.bitcast(xp & jnp.uint32(0xFFFF0000), F32)
        return carry

    lax.fori_loop(0, maxc_ref[t], add_rank, 0)


def _combine(x1, idx, contrib, offs, tokcnt, cap, tn):
    n_tok, d = x1.shape
    nt = n_tok // SEL_T
    lo = jnp.transpose(offs)
    hi = jnp.concatenate([lo[:, 1:], jnp.full((N_EXPERTS, 1), cap, I32)], axis=1)
    maxc = jnp.max(tokcnt[:, 0].reshape(nt, SEL_T), axis=1)
    grid_spec = pltpu.PrefetchScalarGridSpec(
        num_scalar_prefetch=4,
        grid=(nt,),
        in_specs=[
            pl.BlockSpec((SEL_T, d), lambda t, *_: (t, 0)),
            pl.BlockSpec((SEL_T, N_EXPERTS), lambda t, *_: (t, 0)),
            pl.BlockSpec(memory_space=pl.ANY),
        ],
        out_specs=pl.BlockSpec((SEL_T, d), lambda t, *_: (t, 0)),
        scratch_shapes=[pltpu.VMEM((N_EXPERTS, SEL_T, d // 2), U32), pltpu.SemaphoreType.DMA(())],
    )
    return pl.pallas_call(
        functools.partial(_combine_kernel, nt=nt, cap=cap, tn=tn),
        grid_spec=grid_spec,
        out_shape=jax.ShapeDtypeStruct((n_tok, d), F32),
        compiler_params=_params(("arbitrary",), 48),
        name="expert_combine",
    )(lo.reshape(-1), hi.reshape(-1), maxc, idx.reshape(-1), x1, tokcnt,
      contrib.reshape(N_EXPERTS * cap, d // 2))


DOWN_TN = 1024
UP_TF = 256
OUT_TN = 1024


def _tiles(n_tok, n_seq, cap):
    return dict(
        tm=min(512, n_seq),
        rows=min(512, cap),
        tr=min(1024, cap),
    )


def _encoder_layer(x, p):
    b, n, d = x.shape
    n_tok = b * n
    cap = EC_FACTOR * n_tok // N_EXPERTS
    tl = _tiles(n_tok, n, cap)
    x2 = x.reshape(n_tok, d)

    proj, v_b = _in_projection(x2, p["norm1"], p["w_in"], p["gains"], p["cos"][n], p["sin"][n], n, tl["tm"])
    oa = _attention_a(proj, b, n, ATT_TQ)
    ob = _attention_b(proj, v_b, p["sink"], p["bias"], b, n)
    x1 = _out_projection(oa, ob, p["out_norm_a"], p["out_norm_b"], p["w_out"], x2, tl["tm"], OUT_TN)

    hp, aff = _router(x1, p["norm2"], p["w_router"], tl["tm"])
    idx, gates, offs, tokcnt = _select(aff, cap)
    xe = _gather(idx, hp, cap, tl["rows"])
    hid = _ffn_up(xe, p["w_gate"], p["w_up"], tl["tr"], UP_TF)
    contrib = _ffn_down(hid, p["w_down"], gates, tl["tr"], DOWN_TN)
    y = _combine(x1, idx, contrib, offs, tokcnt, cap, DOWN_TN)
    return y.reshape(b, n, d)


def kernel(x_prompt, x_sample, norm1, w_in, q_norm_a, k_norm_a, q_norm_b, k_norm_b, sink_b, rel_bias,
           out_norm_a, out_norm_b, w_out, norm2, w_router, w_gate, w_up, w_down):
    depth = norm1.shape[0]
    seqs = sorted({x_prompt.shape[1], x_sample.shape[1]})
    tables = {n: _rope_tables(n) for n in seqs}
    bias = _window_bias(rel_bias)
    ones = jnp.ones((KVW,), F32)

    layers = []
    for l in range(depth):
        gains = jnp.concatenate([jnp.tile(q_norm_a[l], H_A), jnp.tile(k_norm_a[l], KV_A), ones,
                                 jnp.tile(q_norm_b[l], H_B), jnp.tile(k_norm_b[l], KV_B), ones])
        layers.append(dict(
            norm1=norm1[l], w_in=w_in[l].astype(BF16), gains=gains.reshape(1, IN_WIDTH),
            cos={n: tables[n][0] for n in seqs}, sin={n: tables[n][1] for n in seqs},
            sink=sink_b[l], bias=bias,
            out_norm_a=out_norm_a[l], out_norm_b=out_norm_b[l], w_out=w_out[l].astype(BF16),
            norm2=norm2[l], w_router=w_router[l].astype(BF16),
            w_gate=w_gate[l], w_up=w_up[l], w_down=w_down[l],
        ))

    def run(x):
        for p in layers:
            x = _encoder_layer(x, p)
        return x

    return (run(x_prompt), run(x_sample))
```

```python
import functools
import math

import jax
import jax.numpy as jnp
from jax import lax
from jax.experimental import pallas as pl
from jax.experimental.pallas import tpu as pltpu

F32 = jnp.float32
BF16 = jnp.bfloat16
I32 = jnp.int32
U32 = jnp.uint32

HEAD_DIM = 128
H_A = 16
KV_A = 4
H_B = 16
KV_B = 4
GROUP = 4
W_A = H_A * HEAD_DIM
W_B = H_B * HEAD_DIM
KVW = KV_A * HEAD_DIM
IN_WIDTH = W_A + 2 * KVW + W_B + 2 * KVW
GRID_W = 64
HALF_ROT = HEAD_DIM // 2
ROPE_THETA = 10000.0
REL_BUCKETS = 32
REL_MAX_DIST = 128
WINDOW = 128
N_EXPERTS = 16
EC_FACTOR = 2
NORM_EPS = 1e-6
NEG_INF = -1e30
SCALE = 1.0 / math.sqrt(HEAD_DIM)

V7X_VMEM_BYTES = 64 * 1024 * 1024
LANE = 128


def _vmem(mib):
    assert mib * 1024 * 1024 < V7X_VMEM_BYTES
    return mib * 1024 * 1024


def _params(sem, mib):
    return pltpu.CompilerParams(dimension_semantics=sem, vmem_limit_bytes=_vmem(mib))


IN_TN = 512
_ROPE_TILES = 5
_VA_TILE = 5
_N_TILES = IN_WIDTH // IN_TN
_MAIN_TILES = _N_TILES - 1


def _inproj_kernel(x_ref, g1_ref, w_ref, gain_ref, cos_ref, sin_ref, o_ref, vb_ref, h_ref, raw_ref):
    j = pl.program_id(1)

    @pl.when(j == 0)
    def _():
        x = x_ref[...]
        ms = jnp.mean(x * x, axis=-1, keepdims=True)
        h_ref[...] = (x * lax.rsqrt(ms + NORM_EPS) * g1_ref[...]).astype(BF16)
        raw_ref[...] = jnp.zeros(raw_ref.shape, F32)

    prev = raw_ref[...]
    acc = jnp.dot(h_ref[...], w_ref[...], preferred_element_type=F32)
    jp = j - 1
    is_v = jp == _VA_TILE
    is_rope = jp < _ROPE_TILES
    is_q = jnp.logical_or(jp < W_A // IN_TN,
                          jnp.logical_and(jp > _VA_TILE, jp < (IN_WIDTH - 2 * KVW) // IN_TN))
    outs = []
    for hh in range(IN_TN // HEAD_DIM):
        a = prev[:, hh * HEAD_DIM:(hh + 1) * HEAD_DIM]
        ms = jnp.mean(a * a, axis=-1, keepdims=True)
        r = jnp.where(is_v, 1.0, lax.rsqrt(ms + NORM_EPS))
        outs.append(a * r * gain_ref[:, hh * HEAD_DIM:(hh + 1) * HEAD_DIM])
    y = jnp.concatenate(outs, axis=-1)
    reps = IN_TN // HEAD_DIM
    c = jnp.concatenate([jnp.where(is_rope, cos_ref[...], 1.0)] * reps, axis=-1)
    s = jnp.concatenate([jnp.where(is_rope, sin_ref[...], 0.0)] * reps, axis=-1)
    lane = lax.broadcasted_iota(I32, y.shape, 1)
    first_half = (lane % (2 * (HALF_ROT // 2))) < (HALF_ROT // 2)
    partner = jnp.where(first_half,
                        pltpu.roll(y, IN_TN - HALF_ROT // 2, axis=1),
                        pltpu.roll(y, HALF_ROT // 2, axis=1))
    o_ref[...] = ((y * c + partner * s) * jnp.where(is_q, SCALE * LOG2E, 1.0)).astype(BF16)
    raw_ref[...] = acc

    @pl.when(j == _N_TILES - 1)
    def _():
        vb_ref[...] = acc.astype(BF16)


def _rope_tables(n):
    rows = n // GRID_W
    row = jnp.repeat(jnp.arange(rows, dtype=F32), GRID_W)
    col = jnp.tile(jnp.arange(GRID_W, dtype=F32), rows)
    inv = ROPE_THETA ** (-jnp.arange(0, HALF_ROT, 2, dtype=F32) / HALF_ROT)
    ar, ac = row[:, None] * inv, col[:, None] * inv
    cr, sr, cc, sc = jnp.cos(ar), jnp.sin(ar), jnp.cos(ac), jnp.sin(ac)
    cos_t = jnp.concatenate([cr, cr, cc, cc], axis=-1)
    sin_t = jnp.concatenate([-sr, sr, -sc, sc], axis=-1)
    return cos_t, sin_t


def _in_projection(x2, norm1, w_in_bf, gains, cos_t, sin_t, n_seq, tm):
    n_tok, d = x2.shape
    seq_blocks = n_seq // tm
    return pl.pallas_call(
        _inproj_kernel,
        grid=(n_tok // tm, _N_TILES),
        in_specs=[
            pl.BlockSpec((tm, d), lambda i, j: (i, 0)),
            pl.BlockSpec((1, d), lambda i, j: (0, 0)),
            pl.BlockSpec((d, IN_TN), lambda i, j: (0, j)),
            pl.BlockSpec((1, IN_TN), lambda i, j: (0, jnp.maximum(j - 1, 0))),
            pl.BlockSpec((tm, HEAD_DIM), lambda i, j: (i % seq_blocks, 0)),
            pl.BlockSpec((tm, HEAD_DIM), lambda i, j: (i % seq_blocks, 0)),
        ],
        out_specs=[pl.BlockSpec((tm, IN_TN), lambda i, j: (i, jnp.maximum(j - 1, 0))),
                   pl.BlockSpec((tm, IN_TN), lambda i, j: (i, 0))],
        out_shape=[jax.ShapeDtypeStruct((n_tok, _MAIN_TILES * IN_TN), BF16),
                   jax.ShapeDtypeStruct((n_tok, IN_TN), BF16)],
        scratch_shapes=[pltpu.VMEM((tm, d), BF16), pltpu.VMEM((tm, IN_TN), F32)],
        compiler_params=_params(("parallel", "arbitrary"), 48),
        name="in_projection",
    )(x2, norm1.reshape(1, d), w_in_bf, gains, cos_t, sin_t)


ATT_TQ = 128
ATT_TKC = 256
ATT_UNROLL = 32
LOG2E = 1.4426950408889634


ATT_UNROLL_BOUNDED = 64
ATT_PV_GROUP = 16
ATT_BOUND_LIMIT = 40.0
ATT_BOUND_SLACK = 1.001


def _flash_kernel(q_ref, k_ref, v_ref, o_ref, vt_ref, kmax_ref, *, n_chunks, unroll):
    @pl.when(pl.program_id(2) == 0)
    def _():
        def prepare_chunk(c, kmax2):
            st = pl.multiple_of(c * ATT_TKC, ATT_TKC)
            vt_ref[:, pl.ds(st, ATT_TKC)] = v_ref[pl.ds(st, ATT_TKC), :].T
            kf = k_ref[pl.ds(st, ATT_TKC), :].astype(F32)
            return jnp.maximum(kmax2, jnp.max(jnp.sum(kf * kf, axis=1, keepdims=True), axis=0, keepdims=True))
        kmax2 = lax.fori_loop(0, n_chunks, prepare_chunk, jnp.zeros((1, 1), F32))
        kmax_ref[...] = jnp.broadcast_to(jnp.sqrt(kmax2), kmax_ref.shape)

    q = q_ref[...]
    tq = q.shape[0]
    q_t = jnp.concatenate([q[:, g * HEAD_DIM:(g + 1) * HEAD_DIM].T for g in range(GROUP)], axis=1)
    nq = GROUP * tq
    qf = q_t.astype(F32)
    bound = jnp.sqrt(jnp.sum(qf * qf, axis=0, keepdims=True)) * kmax_ref[...] * ATT_BOUND_SLACK

    def write(acc, l):
        o_t = acc / l
        for g in range(GROUP):
            o_ref[:, g * HEAD_DIM:(g + 1) * HEAD_DIM] = o_t[:, g * tq:(g + 1) * tq].T

    def start(c, u):
        return pl.multiple_of((c * unroll + u) * ATT_TKC, ATT_TKC)

    bounded = jnp.max(bound) <= ATT_BOUND_LIMIT

    @pl.when(bounded)
    def _():
        trip = math.gcd(n_chunks, ATT_UNROLL_BOUNDED)
        group = math.gcd(trip, ATT_PV_GROUP)

        def body(c, carry):
            l, acc = carry
            for g0 in range(0, trip, group):
                first = (c * trip + g0) * ATT_TKC
                probs = []
                for u in range(group):
                    keys = pl.ds(pl.multiple_of(first + u * ATT_TKC, ATT_TKC), ATT_TKC)
                    s = jnp.dot(k_ref[keys, :], q_t, preferred_element_type=F32)
                    p = jnp.exp2(s - bound)
                    l = l + jnp.sum(p, axis=0, keepdims=True)
                    probs.append(p.astype(BF16))
                keys = pl.ds(pl.multiple_of(first, group * ATT_TKC), group * ATT_TKC)
                acc = acc + jnp.dot(vt_ref[:, keys], jnp.concatenate(probs, axis=0), preferred_element_type=F32)
            return l, acc

        init = (jnp.zeros((1, nq), F32), jnp.zeros((HEAD_DIM, nq), F32))
        l, acc = lax.fori_loop(0, n_chunks // trip, body, init)
        write(acc, l)

    @pl.when(jnp.logical_not(bounded))
    def _():
        def body(c, carry):
            m, l, acc = carry
            scores = [jnp.dot(k_ref[pl.ds(start(c, u), ATT_TKC), :], q_t, preferred_element_type=F32)
                      for u in range(unroll)]
            for u, s in enumerate(scores):
                m_new = jnp.maximum(m, jnp.max(s, axis=0, keepdims=True))
                alpha = jnp.exp2(m - m_new)
                p = jnp.exp2(s - m_new)
                l = alpha * l + jnp.sum(p, axis=0, keepdims=True)
                pv = jnp.dot(vt_ref[:, pl.ds(start(c, u), ATT_TKC)], p.astype(BF16), preferred_element_type=F32)
                acc = alpha * acc + pv
                m = m_new
            return m, l, acc

        init = (jnp.full((1, nq), -jnp.inf, F32), jnp.zeros((1, nq), F32), jnp.zeros((HEAD_DIM, nq), F32))
        _, l, acc = lax.fori_loop(0, n_chunks // unroll, body, init)
        write(acc, l)


def _attention_a(proj, b, n, tq):
    n_tok = b * n
    qb = n // tq
    k_col = W_A // HEAD_DIM
    v_col = (W_A + KVW) // HEAD_DIM
    n_chunks = n // ATT_TKC
    unroll = math.gcd(n_chunks, ATT_UNROLL)
    return pl.pallas_call(
        functools.partial(_flash_kernel, n_chunks=n_chunks, unroll=unroll),
        grid=(b, KV_A, qb),
        in_specs=[
            pl.BlockSpec((tq, GROUP * HEAD_DIM), lambda bi, h, qi: (bi * qb + qi, h)),
            pl.BlockSpec((n, HEAD_DIM), lambda bi, h, qi: (bi, k_col + h)),
            pl.BlockSpec((n, HEAD_DIM), lambda bi, h, qi: (bi, v_col + h)),
        ],
        out_specs=pl.BlockSpec((tq, GROUP * HEAD_DIM), lambda bi, h, qi: (bi * qb + qi, h)),
        out_shape=jax.ShapeDtypeStruct((n_tok, W_A), F32),
        scratch_shapes=[pltpu.VMEM((HEAD_DIM, n), BF16), pltpu.VMEM((1, GROUP * tq), F32)],
        compiler_params=_params(("arbitrary", "arbitrary", "arbitrary"), 48),
        name="attention_global",
    )(proj, proj, proj)


WIN_TQ = 2 * WINDOW
WIN_TK = WIN_TQ + 2 * WINDOW


def _t5_buckets(rel):
    nb = REL_BUCKETS // 2
    max_exact = nb // 2
    ret = jnp.where(rel > 0, nb, 0)
    n = jnp.abs(rel)
    nf = jnp.maximum(n, 1).astype(F32)
    large = max_exact + (jnp.log(nf / max_exact) / math.log(REL_MAX_DIST / max_exact)
                         * (nb - max_exact)).astype(I32)
    large = jnp.minimum(large, nb - 1)
    return ret + jnp.where(n < max_exact, n, large)


def _window_bias(rel_bias):
    span = WIN_TQ + WIN_TK
    kk = jnp.arange(span)
    rel = jnp.where(kk < WIN_TK, kk, kk - span) - WINDOW
    tbl = jnp.where((jnp.abs(rel) <= WINDOW)[:, None], rel_bias[_t5_buckets(rel)].astype(F32), NEG_INF)
    flat = jnp.tile(tbl.T, (1, WIN_TQ))[:, :WIN_TQ * (span - 1)]
    bias = flat.reshape(H_B, WIN_TQ, span - 1)[:, :, :WIN_TK]
    bias = bias.reshape(KV_B, GROUP, WIN_TQ, WIN_TK).transpose(0, 3, 1, 2)
    return bias.reshape(KV_B, WIN_TK, GROUP * WIN_TQ) * LOG2E


def _window_kernel(sink_ref, q_ref, kp_ref, kc_ref, kn_ref, vp_ref, vc_ref, vn_ref, bias_ref, o_ref):
    h = pl.program_id(0)
    q = q_ref[...]
    q_all = jnp.concatenate([q[:, g * HEAD_DIM:(g + 1) * HEAD_DIM] for g in range(GROUP)], axis=0)
    kcat = jnp.concatenate([kp_ref[...], kc_ref[...], kn_ref[...]], axis=0)
    vcat = jnp.concatenate([vp_ref[...], vc_ref[...], vn_ref[...]], axis=0)
    s = lax.dot_general(kcat, q_all, (((1,), (1,)), ((), ())), preferred_element_type=F32) + bias_ref[0]
    i = pl.program_id(2)
    before = jnp.where(i == 0, NEG_INF, 0.0)
    after = jnp.where(i == pl.num_programs(2) - 1, NEG_INF, 0.0)
    s = jnp.concatenate([s[:WINDOW] + before, s[WINDOW:WINDOW + WIN_TQ], s[WINDOW + WIN_TQ:] + after], axis=0)
    head = lax.broadcasted_iota(I32, (1, GROUP * WIN_TQ), 1) // WIN_TQ
    snk = jnp.zeros((1, GROUP * WIN_TQ), F32)
    for g in range(GROUP):
        snk = jnp.where(head == g, sink_ref[h * GROUP + g] * LOG2E, snk)
    m = jnp.maximum(jnp.max(s, axis=0, keepdims=True), snk)
    e = jnp.exp2(s - m)
    denom = jnp.sum(e, axis=0, keepdims=True) + jnp.exp2(snk - m)
    o_t = lax.dot_general(vcat, e.astype(BF16), (((0,), (0,)), ((), ())), preferred_element_type=F32) / denom
    for g in range(GROUP):
        o_ref[:, g * HEAD_DIM:(g + 1) * HEAD_DIM] = o_t[:, g * WIN_TQ:(g + 1) * WIN_TQ].T


def _attention_b(proj, v_b, sink, bias, b, n):
    n_tok = b * n
    qb = n // WIN_TQ
    nb128 = n // WINDOW
    base = W_A + 2 * KVW
    q_col = base // (GROUP * HEAD_DIM)
    k_col = (base + W_B) // HEAD_DIM
    v_col = 0

    def prev_map(col):
        return lambda h, bi, i, s: (bi * nb128 + jnp.maximum(2 * i - 1, 0), col + h)

    def cur_map(col):
        return lambda h, bi, i, s: (bi * qb + i, col + h)

    def next_map(col):
        return lambda h, bi, i, s: (bi * nb128 + jnp.minimum(2 * i + 2, nb128 - 1), col + h)

    grid_spec = pltpu.PrefetchScalarGridSpec(
        num_scalar_prefetch=1,
        grid=(KV_B, b, qb),
        in_specs=[
            pl.BlockSpec((WIN_TQ, GROUP * HEAD_DIM), lambda h, bi, i, s: (bi * qb + i, q_col + h)),
            pl.BlockSpec((WINDOW, HEAD_DIM), prev_map(k_col)),
            pl.BlockSpec((WIN_TQ, HEAD_DIM), cur_map(k_col)),
            pl.BlockSpec((WINDOW, HEAD_DIM), next_map(k_col)),
            pl.BlockSpec((WINDOW, HEAD_DIM), prev_map(v_col)),
            pl.BlockSpec((WIN_TQ, HEAD_DIM), cur_map(v_col)),
            pl.BlockSpec((WINDOW, HEAD_DIM), next_map(v_col)),
            pl.BlockSpec((1, WIN_TK, GROUP * WIN_TQ), lambda h, bi, i, s: (h, 0, 0)),
        ],
        out_specs=pl.BlockSpec((WIN_TQ, GROUP * HEAD_DIM), lambda h, bi, i, s: (bi * qb + i, h)),
    )
    return pl.pallas_call(
        _window_kernel,
        grid_spec=grid_spec,
        out_shape=jax.ShapeDtypeStruct((n_tok, W_B), F32),
        compiler_params=_params(("parallel", "parallel", "parallel"), 32),
        name="attention_window",
    )(sink, proj, proj, proj, proj, v_b, v_b, v_b, bias)


def _outproj_kernel(oa_ref, ob_ref, ga_ref, gb_ref, w_ref, x_ref, o_ref, mix_ref):
    j = pl.program_id(1)

    @pl.when(j == 0)
    def _():
        for src, gain, lo in ((oa_ref, ga_ref, 0), (ob_ref, gb_ref, W_A)):
            o = src[...]
            ms = jnp.mean(o * o, axis=-1, keepdims=True)
            mix_ref[:, lo:lo + o.shape[1]] = (o * lax.rsqrt(ms + NORM_EPS) * gain[...]).astype(BF16)

    o_ref[...] = x_ref[...] + jnp.dot(mix_ref[...], w_ref[...], preferred_element_type=F32)


def _out_projection(oa, ob, gain_a, gain_b, w_out_bf, x2, tm, tn):
    n_tok, d = x2.shape
    return pl.pallas_call(
        _outproj_kernel,
        grid=(n_tok // tm, d // tn),
        in_specs=[
            pl.BlockSpec((tm, W_A), lambda i, j: (i, 0)),
            pl.BlockSpec((tm, W_B), lambda i, j: (i, 0)),
            pl.BlockSpec((1, W_A), lambda i, j: (0, 0)),
            pl.BlockSpec((1, W_B), lambda i, j: (0, 0)),
            pl.BlockSpec((W_A + W_B, tn), lambda i, j: (0, j)),
            pl.BlockSpec((tm, tn), lambda i, j: (i, j)),
        ],
        out_specs=pl.BlockSpec((tm, tn), lambda i, j: (i, j)),
        out_shape=jax.ShapeDtypeStruct((n_tok, d), F32),
        scratch_shapes=[pltpu.VMEM((tm, W_A + W_B), BF16)],
        compiler_params=_params(("parallel", "arbitrary"), 48),
        name="out_projection",
    )(oa, ob, gain_a.reshape(1, W_A), gain_b.reshape(1, W_B), w_out_bf, x2)


def _router_kernel(x_ref, g_ref, wr_ref, hp_ref, aff_ref):
    x = x_ref[...]
    ms = jnp.mean(x * x, axis=-1, keepdims=True)
    hb = (x * lax.rsqrt(ms + NORM_EPS) * g_ref[...]).astype(BF16)
    logits = jnp.dot(hb, wr_ref[...], preferred_element_type=F32)
    e = jnp.exp(logits - jnp.max(logits, axis=-1, keepdims=True))
    aff_ref[...] = e / jnp.sum(e, axis=-1, keepdims=True)
    half = hb.shape[1] // 2
    bits = pltpu.bitcast(hb.astype(F32), U32)
    hp_ref[...] = (bits[:, :half] >> 16) | (bits[:, half:] & jnp.uint32(0xFFFF0000))


def _router(x1, norm2, w_router_bf, tm):
    n_tok, d = x1.shape
    return pl.pallas_call(
        _router_kernel,
        grid=(n_tok // tm,),
        in_specs=[
            pl.BlockSpec((tm, d), lambda i: (i, 0)),
            pl.BlockSpec((1, d), lambda i: (0, 0)),
            pl.BlockSpec((d, N_EXPERTS), lambda i: (0, 0)),
        ],
        out_specs=[pl.BlockSpec((tm, d // 2), lambda i: (i, 0)),
                   pl.BlockSpec((tm, N_EXPERTS), lambda i: (i, 0))],
        out_shape=[jax.ShapeDtypeStruct((n_tok, d // 2), U32),
                   jax.ShapeDtypeStruct((n_tok, N_EXPERTS), F32)],
        compiler_params=_params(("parallel",), 48),
        name="router",
    )(x1, norm2.reshape(1, d), w_router_bf)


SEL_T = 128
SEL_CHUNK = 1024
RANK_BITS = 4
assert N_EXPERTS <= 1 << RANK_BITS


def _slots_kernel(aff_ref, slot_ref, tokrank_ref, tokcnt_ref, off_ref, thr_ref, need_ref, carry_ref, *, cap, n_tok):
    t = pl.program_id(0)
    n_chunks = n_tok // SEL_CHUNK

    def count(pred_fn):
        def body(c, tot):
            bits = pltpu.bitcast(aff_ref[pl.ds(c * SEL_CHUNK, SEL_CHUNK), :], I32)
            return tot + jnp.sum(pred_fn(bits).astype(I32), axis=0, keepdims=True)
        return lax.fori_loop(0, n_chunks, body, jnp.zeros((1, N_EXPERTS), I32))

    @pl.when(t == 0)
    def _():
        def bit_body(b, thr):
            cand = thr | jnp.left_shift(jnp.int32(1), 30 - b)
            return jnp.where(count(lambda bits: bits >= cand) >= cap, cand, thr)
        thr = lax.fori_loop(0, 31, bit_body, jnp.zeros((1, N_EXPERTS), I32))
        thr_ref[...] = thr
        need_ref[...] = (cap - count(lambda bits: bits > thr)).astype(F32)
        carry_ref[...] = jnp.zeros(carry_ref.shape, F32)

    a = aff_ref[pl.ds(t * SEL_T, SEL_T), :]
    bits = pltpu.bitcast(a, I32)
    thr = thr_ref[...]
    gt = bits > thr
    eq = bits == thr
    r = lax.broadcasted_iota(I32, (SEL_T, SEL_T), 0)
    c = lax.broadcasted_iota(I32, (SEL_T, SEL_T), 1)
    tri = jnp.where(r >= c, 1.0, 0.0).astype(BF16)
    eq_f = jnp.where(eq, 1.0, 0.0)
    cum_eq = jnp.dot(tri, eq_f.astype(BF16), preferred_element_type=F32) + carry_ref[0:1, :]
    sel = jnp.logical_or(gt, jnp.logical_and(eq, cum_eq <= need_ref[...]))
    sel_f = jnp.where(sel, 1.0, 0.0)
    pos = jnp.dot(tri, sel_f.astype(BF16), preferred_element_type=F32) + carry_ref[1:2, :]
    off_ref[0] = carry_ref[1:2, :].astype(I32)
    carry_ref[0:1, :] = carry_ref[0:1, :] + jnp.sum(eq_f, axis=0, keepdims=True)
    carry_ref[1:2, :] = carry_ref[1:2, :] + jnp.sum(sel_f, axis=0, keepdims=True)
    slot_ref[...] = jnp.where(sel, pos - 1.0, -1.0).astype(I32)
    er = lax.broadcasted_iota(I32, (N_EXPERTS, N_EXPERTS), 0)
    ec = lax.broadcasted_iota(I32, (N_EXPERTS, N_EXPERTS), 1)
    before = jnp.where(er < ec, 1.0, 0.0).astype(BF16)
    rank = jnp.dot(sel_f.astype(BF16), before, preferred_element_type=F32).astype(I32)
    tok = t * SEL_T + lax.broadcasted_iota(I32, (SEL_T, N_EXPERTS), 0)
    tokrank_ref[...] = tok * (1 << RANK_BITS) + rank
    tokcnt_ref[...] = jnp.broadcast_to(jnp.sum(sel_f, axis=1, keepdims=True).astype(I32), (SEL_T, N_EXPERTS))


def _lists_kernel(off_ref, slot_ref, tokrank_ref, aff_ref, idx_ref, gate_ref):
    t = pl.program_id(0)

    @pl.when(t == 0)
    def _():
        idx_ref[...] = jnp.zeros(idx_ref.shape, I32)
        gate_ref[...] = jnp.zeros(gate_ref.shape, F32)

    slot = slot_ref[...]
    tokrank = tokrank_ref[...]
    a = aff_ref[...]
    for e in range(N_EXPERTS):
        base = pl.multiple_of((off_ref[t * N_EXPERTS + e] // LANE) * LANE, LANE)
        lane = base + lax.broadcasted_iota(I32, (1, 2 * LANE), 1)
        match = slot[:, e:e + 1] == lane
        win = (slice(e, e + 1), pl.ds(base, 2 * LANE))
        idx_ref[win] += jnp.sum(jnp.where(match, tokrank[:, e:e + 1], 0), axis=0, keepdims=True)
        gate_ref[win] += jnp.sum(jnp.where(match, a[:, e:e + 1], 0.0), axis=0, keepdims=True)


def _select(aff, cap):
    n_tok = aff.shape[0]
    nt = n_tok // SEL_T
    tile = pl.BlockSpec((SEL_T, N_EXPERTS), lambda t: (t, 0))
    tile_shape = jax.ShapeDtypeStruct((n_tok, N_EXPERTS), I32)
    slot, tokrank, tokcnt, offs = pl.pallas_call(
        functools.partial(_slots_kernel, cap=cap, n_tok=n_tok),
        grid=(nt,),
        in_specs=[pl.BlockSpec((n_tok, N_EXPERTS), lambda t: (0, 0))],
        out_specs=[tile, tile, tile, pl.BlockSpec((1, 1, N_EXPERTS), lambda t: (t, 0, 0))],
        out_shape=[tile_shape, tile_shape, tile_shape, jax.ShapeDtypeStruct((nt, 1, N_EXPERTS), I32)],
        scratch_shapes=[pltpu.VMEM((1, N_EXPERTS), I32), pltpu.VMEM((1, N_EXPERTS), F32),
                        pltpu.VMEM((2, N_EXPERTS), F32)],
        compiler_params=_params(("arbitrary",), 32),
        name="expert_slots",
    )(aff)
    offs = offs.reshape(nt, N_EXPERTS)
    width = cap + 2 * LANE
    tile_p = pl.BlockSpec((SEL_T, N_EXPERTS), lambda t, o: (t, 0))
    idx, gates = pl.pallas_call(
        _lists_kernel,
        grid_spec=pltpu.PrefetchScalarGridSpec(
            num_scalar_prefetch=1, grid=(nt,),
            in_specs=[tile_p, tile_p, tile_p],
            out_specs=[pl.BlockSpec((N_EXPERTS, width), lambda t, o: (0, 0)),
                       pl.BlockSpec((N_EXPERTS, width), lambda t, o: (0, 0))]),
        out_shape=[jax.ShapeDtypeStruct((N_EXPERTS, width), I32),
                   jax.ShapeDtypeStruct((N_EXPERTS, width), F32)],
        compiler_params=_params(("arbitrary",), 32),
        name="expert_lists",
    )(offs.reshape(-1), slot, tokrank, aff)
    return idx[:, :cap], gates[:, :cap], offs, tokcnt


def _gather_kernel(idx_ref, hp_ref, xe_ref, sem, *, rows):
    def row_copy(s):
        tok = idx_ref[0, 0, s] >> RANK_BITS
        return pltpu.make_async_copy(hp_ref.at[pl.ds(tok, 1)], xe_ref.at[0, pl.ds(s, 1)], sem)

    def start(s, carry):
        row_copy(s).start()
        return carry

    def wait(s, carry):
        row_copy(s).wait()
        return carry

    lax.fori_loop(0, rows, start, 0, unroll=GATHER_UNROLL)
    lax.fori_loop(0, rows, wait, 0, unroll=GATHER_UNROLL)


GATHER_UNROLL = 8


def _gather(idx, hp, cap, rows):
    half = hp.shape[1]
    n_blocks = cap // rows
    idx3 = idx.reshape(N_EXPERTS * n_blocks, 1, rows)
    return pl.pallas_call(
        functools.partial(_gather_kernel, rows=rows),
        grid=(N_EXPERTS, n_blocks),
        in_specs=[pl.BlockSpec((1, 1, rows), lambda e, r: (e * n_blocks + r, 0, 0), memory_space=pltpu.SMEM),
                  pl.BlockSpec(memory_space=pl.ANY)],
        out_specs=pl.BlockSpec((1, rows, half), lambda e, r: (e, r, 0)),
        out_shape=jax.ShapeDtypeStruct((N_EXPERTS, cap, half), U32),
        scratch_shapes=[pltpu.SemaphoreType.DMA(())],
        compiler_params=_params(("arbitrary", "arbitrary"), 32),
        name="expert_gather",
    )(idx3, hp)


def _unpack(xp):
    lo = pltpu.bitcast(xp << 16, F32).astype(BF16)
    hi = pltpu.bitcast(xp & jnp.uint32(0xFFFF0000), F32).astype(BF16)
    return lo, hi


def _ffn_up_kernel(xe_hbm, wg_ref, wu_ref, h_ref, xe_vmem, wg_bf, wu_bf, sem, *, tr):
    e, j, r = pl.program_id(0), pl.program_id(1), pl.program_id(2)

    @pl.when(jnp.logical_and(j == 0, r == 0))
    def _():
        copy = pltpu.make_async_copy(xe_hbm.at[e], xe_vmem, sem)
        copy.start()
        copy.wait()

    @pl.when(r == 0)
    def _():
        wg_bf[...] = wg_ref[0].astype(BF16)
        wu_bf[...] = wu_ref[0].astype(BF16)

    lo, hi = _unpack(xe_vmem[pl.ds(pl.multiple_of(r * tr, tr), tr), :])
    half = lo.shape[1]

    def proj(w_bf):
        return (jnp.dot(lo, w_bf[:half, :], preferred_element_type=F32)
                + jnp.dot(hi, w_bf[half:, :], preferred_element_type=F32))

    gate = proj(wg_bf)
    up = proj(wu_bf)
    h_ref[0] = (gate * jax.nn.sigmoid(gate) * up).astype(BF16)


def _ffn_up(xe, w_gate, w_up, tr, tf):
    _, cap, half = xe.shape
    d, f = w_gate.shape[1], w_gate.shape[2]
    return pl.pallas_call(
        functools.partial(_ffn_up_kernel, tr=tr),
        grid=(N_EXPERTS, f // tf, cap // tr),
        in_specs=[pl.BlockSpec(memory_space=pl.ANY),
                  pl.BlockSpec((1, d, tf), lambda e, j, r: (e, 0, j)),
                  pl.BlockSpec((1, d, tf), lambda e, j, r: (e, 0, j))],
        out_specs=pl.BlockSpec((1, tr, tf), lambda e, j, r: (e, r, j)),
        out_shape=jax.ShapeDtypeStruct((N_EXPERTS, cap, f), BF16),
        scratch_shapes=[pltpu.VMEM((cap, half), U32), pltpu.VMEM((d, tf), BF16), pltpu.VMEM((d, tf), BF16),
                        pltpu.SemaphoreType.DMA(())],
        compiler_params=_params(("arbitrary", "arbitrary", "arbitrary"), 56),
        name="expert_ffn_up",
    )(xe, w_gate, w_up)


def _pack_halves(x):
    w = x.shape[1] // 2
    bits = pltpu.bitcast(x.astype(BF16).astype(F32), U32)
    return (bits[:, :w] >> 16) | (bits[:, w:] & jnp.uint32(0xFFFF0000))


def _ffn_down_kernel(h_ref, wd_ref, g_ref, o_ref, wd_bf, *, tr):
    wd_bf[...] = wd_ref[0].astype(BF16)

    def row_tile(r, carry):
        rows = pl.ds(pl.multiple_of(r * tr, tr), tr)
        out = jnp.dot(h_ref[0, rows, :], wd_bf[...], preferred_element_type=F32)
        o_ref[0, rows, :] = _pack_halves(g_ref[0, rows, :] * out)
        return carry

    lax.fori_loop(0, h_ref.shape[1] // tr, row_tile, 0)


def _ffn_down(hid, w_down, gates, tr, tn):
    _, cap, f = hid.shape
    d = w_down.shape[2]
    return pl.pallas_call(
        functools.partial(_ffn_down_kernel, tr=tr),
        grid=(N_EXPERTS, d // tn),
        in_specs=[pl.BlockSpec((1, cap, f), lambda e, j: (e, 0, 0)),
                  pl.BlockSpec((1, f, tn), lambda e, j: (e, 0, j)),
                  pl.BlockSpec((1, cap, 1), lambda e, j: (e, 0, 0))],
        out_specs=pl.BlockSpec((1, cap, tn // 2), lambda e, j: (e, 0, j)),
        out_shape=jax.ShapeDtypeStruct((N_EXPERTS, cap, d // 2), U32),
        scratch_shapes=[pltpu.VMEM((f, tn), BF16)],
        compiler_params=_params(("parallel", "arbitrary"), 56),
        name="expert_ffn_down",
    )(hid, w_down, gates.reshape(N_EXPERTS, cap, 1))


def _combine_kernel(lo_ref, hi_ref, maxc_ref, idx_ref, x_ref, cnt_ref, con_ref, y_ref, stage_ref, sem,
                    *, nt, cap, tn):
    t = pl.program_id(0)

    @pl.when(t == 0)
    def _():
        stage_ref[...] = jnp.zeros(stage_ref.shape, U32)

    def row_copy(row):
        v = idx_ref[row]
        tok = (v >> RANK_BITS) - t * SEL_T
        return pltpu.make_async_copy(con_ref.at[pl.ds(row, 1)],
                                     stage_ref.at[v & ((1 << RANK_BITS) - 1), pl.ds(tok, 1)], sem)

    def for_rows(fn):
        def per_row(row, carry):
            fn(row_copy(row))
            return carry

        def per_expert(e, carry):
            k = e * nt + t
            return lax.fori_loop(e * cap + lo_ref[k], e * cap + hi_ref[k], per_row, carry)

        lax.fori_loop(0, N_EXPERTS, per_expert, 0)

    for_rows(lambda copy: copy.start())
    y_ref[...] = x_ref[...]
    for_rows(lambda copy: copy.wait())

    half = tn // 2
    cnt = jnp.broadcast_to(cnt_ref[:, 0:1], (SEL_T, half))

    def add_rank(k, carry):
        live = k < cnt
        for j in range(y_ref.shape[1] // tn):
            xp = jnp.where(live, stage_ref[k, :, j * half:(j + 1) * half], jnp.uint32(0))
            y_ref[:, j * tn:j * tn + half] += pltpu.bitcast(xp << 16, F32)
            y_ref[:, j * tn + half:(j + 1) * tn] += pltpu.bitcast(xp & jnp.uint32(0xFFFF0000), F32)
        return carry

    lax.fori_loop(0, maxc_ref[t], add_rank, 0)


def _combine(x1, idx, contrib, offs, tokcnt, cap, tn):
    n_tok, d = x1.shape
    nt = n_tok // SEL_T
    lo = jnp.transpose(offs)
    hi = jnp.concatenate([lo[:, 1:], jnp.full((N_EXPERTS, 1), cap, I32)], axis=1)
    maxc = jnp.max(tokcnt[:, 0].reshape(nt, SEL_T), axis=1)
    grid_spec = pltpu.PrefetchScalarGridSpec(
        num_scalar_prefetch=4,
        grid=(nt,),
        in_specs=[
            pl.BlockSpec((SEL_T, d), lambda t, *_: (t, 0)),
            pl.BlockSpec((SEL_T, N_EXPERTS), lambda t, *_: (t, 0)),
            pl.BlockSpec(memory_space=pl.ANY),
        ],
        out_specs=pl.BlockSpec((SEL_T, d), lambda t, *_: (t, 0)),
        scratch_shapes=[pltpu.VMEM((N_EXPERTS, SEL_T, d // 2), U32), pltpu.SemaphoreType.DMA(())],
    )
    return pl.pallas_call(
        functools.partial(_combine_kernel, nt=nt, cap=cap, tn=tn),
        grid_spec=grid_spec,
        out_shape=jax.ShapeDtypeStruct((n_tok, d), F32),
        compiler_params=_params(("arbitrary",), 48),
        name="expert_combine",
    )(lo.reshape(-1), hi.reshape(-1), maxc, idx.reshape(-1), x1, tokcnt,
      contrib.reshape(N_EXPERTS * cap, d // 2))


DOWN_TN = 1024
UP_TF = 256
OUT_TN = 1024


def _tiles(n_tok, n_seq, cap):
    return dict(
        tm=min(512, n_seq),
        rows=min(512, cap),
        tr=min(1024, cap),
    )


def _encoder_layer(x, p):
    b, n, d = x.shape
    n_tok = b * n
    cap = EC_FACTOR * n_tok // N_EXPERTS
    tl = _tiles(n_tok, n, cap)
    x2 = x.reshape(n_tok, d)

    proj, v_b = _in_projection(x2, p["norm1"], p["w_in"], p["gains"], p["cos"][n], p["sin"][n], n, tl["tm"])
    oa = _attention_a(proj, b, n, ATT_TQ)
    ob = _attention_b(proj, v_b, p["sink"], p["bias"], b, n)
    x1 = _out_projection(oa, ob, p["out_norm_a"], p["out_norm_b"], p["w_out"], x2, tl["tm"], OUT_TN)

    hp, aff = _router(x1, p["norm2"], p["w_router"], tl["tm"])
    idx, gates, offs, tokcnt = _select(aff, cap)
    xe = _gather(idx, hp, cap, tl["rows"])
    hid = _ffn_up(xe, p["w_gate"], p["w_up"], tl["tr"], UP_TF)
    contrib = _ffn_down(hid, p["w_down"], gates, tl["tr"], DOWN_TN)
    y = _combine(x1, idx, contrib, offs, tokcnt, cap, DOWN_TN)
    return y.reshape(b, n, d)


def kernel(x_prompt, x_sample, norm1, w_in, q_norm_a, k_norm_a, q_norm_b, k_norm_b, sink_b, rel_bias,
           out_norm_a, out_norm_b, w_out, norm2, w_router, w_gate, w_up, w_down):
    depth = norm1.shape[0]
    seqs = sorted({x_prompt.shape[1], x_sample.shape[1]})
    tables = {n: _rope_tables(n) for n in seqs}
    bias = _window_bias(rel_bias)
    ones = jnp.ones((KVW,), F32)

    layers = []
    for l in range(depth):
        gains = jnp.concatenate([jnp.tile(q_norm_a[l], H_A), jnp.tile(k_norm_a[l], KV_A), ones,
                                 jnp.tile(q_norm_b[l], H_B), jnp.tile(k_norm_b[l], KV_B), ones])
        layers.append(dict(
            norm1=norm1[l], w_in=w_in[l].astype(BF16), gains=gains.reshape(1, IN_WIDTH),
            cos={n: tables[n][0] for n in seqs}, sin={n: tables[n][1] for n in seqs},
            sink=sink_b[l], bias=bias,
            out_norm_a=out_norm_a[l], out_norm_b=out_norm_b[l], w_out=w_out[l].astype(BF16),
            norm2=norm2[l], w_router=w_router[l].astype(BF16),
            w_gate=w_gate[l], w_up=w_up[l], w_down=w_down[l],
        ))

    def run(x):
        for p in layers:
            x = _encoder_layer(x, p)
        return x

    return (run(x_prompt), run(x_sample))
```
